```python
import math
import jax, jax.numpy as jnp
from jax import lax
import numpy as np

D_MODEL = 2048
BATCH = 4
SEQ = 2048
DEPTH = 4
DEC_BATCH = 128
DEC_SEQ = 8
PAST_LEN = 8192
PAGE_SIZE = 128

N_MIXERS = 2
N_GLA_LAYERS = (DEPTH + 1) // 2
N_MLA_LAYERS = DEPTH // 2

GLA_HEADS = 4
GLA_DK = D_MODEL // 2 // GLA_HEADS
GLA_DV = D_MODEL // GLA_HEADS
GLA_GATE_RANK = 16
GLA_GATE_TEMP = 16.0
GLA_CHUNK = 64
GLA_IN = 2 * GLA_HEADS * GLA_DK + 2 * GLA_HEADS * GLA_DV + GLA_GATE_RANK

MLA_HEADS = 16
MLA_Q_LORA = 512
MLA_KV_LORA = 512
MLA_NOPE = 128
MLA_ROPE = 64
MLA_V = 128
MLA_IN = MLA_Q_LORA + MLA_KV_LORA + MLA_ROPE
ROPE_THETA = 10000.0
ATTN_Q_BLOCK = 128

D_FF = -(-8 * D_MODEL // (3 * 256)) * 256
RMS_EPS = 1e-6

kernel_name = "hybrid_gla_mla_decoder_step"


def rmsnorm(x, g):
    xf = x.astype(jnp.float32)
    y = xf * lax.rsqrt(jnp.mean(xf * xf, axis=-1, keepdims=True) + RMS_EPS)
    return (y * g.astype(jnp.float32)).astype(x.dtype)


def rope(x, pos):
    half = MLA_ROPE // 2
    inv = 1.0 / (ROPE_THETA ** (jnp.arange(half, dtype=jnp.float32) / half))
    ang = pos.astype(jnp.float32)[:, None] * inv[None, :]
    shape = (1, pos.shape[0]) + (1,) * (x.ndim - 3) + (half,)
    cos = jnp.cos(ang).reshape(shape)
    sin = jnp.sin(ang).reshape(shape)
    xf = x.astype(jnp.float32)
    x1, x2 = xf[..., :half], xf[..., half:]
    return jnp.concatenate([x1 * cos - x2 * sin, x1 * sin + x2 * cos], axis=-1).astype(x.dtype)


def gla_recurrence(q, k, v, log_a, s0):
    B, L = q.shape[:2]
    C = math.gcd(GLA_CHUNK, L)
    n = L // C

    def to_chunks(t):
        return jnp.moveaxis(t.astype(jnp.float32).reshape((B, n, C) + t.shape[2:]), 1, 0)

    causal = jnp.tril(jnp.ones((C, C), dtype=bool))[None, :, :, None, None]

    def step(s, xs):
        qc, kc, vc, ac = xs
        b = jnp.cumsum(ac, axis=1)
        b_last = b[:, -1]
        o_inter = jnp.einsum('bchk,bhkv->bchv', qc * jnp.exp(b), s)
        diff = b[:, :, None] - b[:, None, :]
        decay = jnp.exp(jnp.where(causal, diff, -jnp.inf))
        att = jnp.einsum('bihk,bjhk,bijhk->bhij', qc, kc, decay)
        o_intra = jnp.einsum('bhij,bjhv->bihv', att, vc)
        s_new = jnp.exp(b_last)[..., None] * s + jnp.einsum(
            'bjhk,bjhv->bhkv', kc * jnp.exp(b_last[:, None] - b), vc)
        return s_new, o_inter + o_intra

    s_fin, o = lax.scan(step, s0.astype(jnp.float32),
                        (to_chunks(q), to_chunks(k), to_chunks(v), to_chunks(log_a)))
    o = jnp.moveaxis(o, 0, 1).reshape(B, L, q.shape[2], v.shape[-1])
    return o, s_fin


def gla_mixer(h, s0, w_in, w_gate_up, b_gate, g_norm, w_out):
    B, L, _ = h.shape
    nq = GLA_HEADS * GLA_DK
    nv = GLA_HEADS * GLA_DV
    q, k, v, r, a_lr = jnp.split(h @ w_in, [nq, 2 * nq, 2 * nq + nv, 2 * nq + 2 * nv], axis=-1)
    q = q.reshape(B, L, GLA_HEADS, GLA_DK) * (GLA_DK ** -0.5)
    k = k.reshape(B, L, GLA_HEADS, GLA_DK)
    v = v.reshape(B, L, GLA_HEADS, GLA_DV)
    log_a = jax.nn.log_sigmoid((a_lr @ w_gate_up + b_gate).astype(jnp.float32)) / GLA_GATE_TEMP
    log_a = log_a.reshape(B, L, GLA_HEADS, GLA_DK)
    o, s = gla_recurrence(q, k, v, log_a, s0)
    o = rmsnorm(o, g_norm).astype(h.dtype) * jax.nn.silu(r).reshape(B, L, GLA_HEADS, GLA_DV)
    return o.reshape(B, L, nv) @ w_out, s.astype(s0.dtype)


def latent_attention(q_lat, q_rope, c_segs, kr_segs, kpos_segs, q_pos):
    B, L, H, R = q_lat.shape
    QB = math.gcd(ATTN_Q_BLOCK, L)
    nb = L // QB
    scale = (MLA_NOPE + MLA_ROPE) ** -0.5

    def to_blocks(t):
        return jnp.moveaxis(t.reshape((B, nb, QB) + t.shape[2:]), 1, 0)

    def blk(xs):
        ql, qr, qp = xs
        scores = []
        for c, kr, kp in zip(c_segs, kr_segs, kpos_segs):
            s = (jnp.einsum('bqhr,btr->bhqt', ql, c)
                 + jnp.einsum('bqhe,bte->bhqt', qr, kr)).astype(jnp.float32) * scale
            scores.append(jnp.where((kp[None, :] <= qp[:, None])[None, None], s, -jnp.inf))
        p = jax.nn.softmax(jnp.concatenate(scores, axis=-1), axis=-1).astype(q_lat.dtype)
        out = 0.0
        off = 0
        for c in c_segs:
            t = c.shape[1]
            out = out + jnp.einsum('bhqt,btr->bqhr', p[..., off:off + t], c)
            off += t
        return out

    o = lax.map(blk, (to_blocks(q_lat), to_blocks(q_rope), q_pos.reshape(nb, QB)))
    return jnp.moveaxis(o, 0, 1).reshape(B, L, H, R)


def mla_mixer(h, pos, past_c, past_kr, past_pos, w_in, g_q, w_qb, g_kv, w_uk, w_uv, w_out):
    B, L, _ = h.shape
    cq, ckv, kr = jnp.split(h @ w_in, [MLA_Q_LORA, MLA_Q_LORA + MLA_KV_LORA], axis=-1)
    q = (rmsnorm(cq, g_q) @ w_qb).reshape(B, L, MLA_HEADS, MLA_NOPE + MLA_ROPE)
    q_nope = q[..., :MLA_NOPE]
    q_rope = rope(q[..., MLA_NOPE:], pos)
    c_new = rmsnorm(ckv, g_kv)
    kr_new = rope(kr, pos)
    q_lat = jnp.einsum('blhn,rhn->blhr', q_nope, w_uk)
    if past_c is None:
        segs = ((c_new,), (kr_new,), (pos,))
    else:
        segs = ((past_c, c_new), (past_kr, kr_new), (past_pos, pos))
    o_lat = latent_attention(q_lat, q_rope, segs[0], segs[1], segs[2], pos)
    o = jnp.einsum('blhr,rhv->blhv', o_lat, w_uv).reshape(B, L, MLA_HEADS * MLA_V)
    return o @ w_out, c_new, kr_new


def swiglu(h, w_gate, w_up, w_down):
    return (jax.nn.silu(h @ w_gate) * (h @ w_up)) @ w_down


def setup_inputs(seed: int = 0) -> dict:
    key = jax.random.key(seed)
    ks = jax.random.split(key, 32)
    f32 = jnp.float32
    n_pages = PAST_LEN // PAGE_SIZE
    n_pool = (DEC_BATCH * n_pages * 5) // 4

    def nrm(k, shape, fan_in):
        return jax.random.normal(k, shape, f32) * (fan_in ** -0.5)

    def gain(k, shape):
        return 1.0 + 0.01 * jax.random.normal(k, shape, f32)

    page_table = jax.random.permutation(ks[5], n_pool)[:DEC_BATCH * n_pages]
    page_table = page_table.reshape(DEC_BATCH, n_pages).astype(jnp.int32)
    return {
        "x_prompt": jax.random.normal(ks[0], (BATCH, SEQ, D_MODEL), f32),
        "x_sample": jax.random.normal(ks[1], (DEC_BATCH, DEC_SEQ, D_MODEL), f32),
        "state_gla": jax.random.normal(ks[2], (N_GLA_LAYERS, DEC_BATCH, GLA_HEADS, GLA_DK, GLA_DV), f32),
        "cache_mla_latent": jax.random.normal(ks[3], (N_MLA_LAYERS, n_pool, PAGE_SIZE, MLA_KV_LORA), f32),
        "cache_mla_krope": jax.random.normal(ks[4], (N_MLA_LAYERS, n_pool, PAGE_SIZE, MLA_ROPE), f32),
        "page_table": page_table,
        "ln_mixer_g": gain(ks[6], (DEPTH, D_MODEL)),
        "ln_ffn_g": gain(ks[7], (DEPTH, D_MODEL)),
        "gla_w_in": nrm(ks[8], (N_GLA_LAYERS, D_MODEL, GLA_IN), D_MODEL),
        "gla_w_gate_up": nrm(ks[9], (N_GLA_LAYERS, GLA_GATE_RANK, GLA_HEADS * GLA_DK), GLA_GATE_RANK),
        "gla_b_gate": 0.1 * jax.random.normal(ks[10], (N_GLA_LAYERS, GLA_HEADS * GLA_DK), f32),
        "gla_norm_g": gain(ks[11], (N_GLA_LAYERS, GLA_DV)),
        "gla_w_out": nrm(ks[12], (N_GLA_LAYERS, GLA_HEADS * GLA_DV, D_MODEL), GLA_HEADS * GLA_DV),
        "mla_w_in": nrm(ks[13], (N_MLA_LAYERS, D_MODEL, MLA_IN), D_MODEL),
        "mla_q_norm_g": gain(ks[14], (N_MLA_LAYERS, MLA_Q_LORA)),
        "mla_w_qb": nrm(ks[15], (N_MLA_LAYERS, MLA_Q_LORA, MLA_HEADS * (MLA_NOPE + MLA_ROPE)), MLA_Q_LORA),
        "mla_kv_norm_g": gain(ks[16], (N_MLA_LAYERS, MLA_KV_LORA)),
        "mla_w_uk": nrm(ks[17], (N_MLA_LAYERS, MLA_KV_LORA, MLA_HEADS, MLA_NOPE), MLA_KV_LORA),
        "mla_w_uv": nrm(ks[18], (N_MLA_LAYERS, MLA_KV_LORA, MLA_HEADS, MLA_V), MLA_KV_LORA),
        "mla_w_out": nrm(ks[19], (N_MLA_LAYERS, MLA_HEADS * MLA_V, D_MODEL), MLA_HEADS * MLA_V),
        "ffn_w_gate": nrm(ks[20], (DEPTH, D_MODEL, D_FF), D_MODEL),
        "ffn_w_up": nrm(ks[21], (DEPTH, D_MODEL, D_FF), D_MODEL),
        "ffn_w_down": nrm(ks[22], (DEPTH, D_FF, D_MODEL), D_FF),
        "final_norm_g": gain(ks[23], (D_MODEL,)),
    }


def reference(x_prompt, x_sample, state_gla, cache_mla_latent, cache_mla_krope, page_table,
              ln_mixer_g, ln_ffn_g, gla_w_in, gla_w_gate_up, gla_b_gate, gla_norm_g, gla_w_out,
              mla_w_in, mla_q_norm_g, mla_w_qb, mla_kv_norm_g, mla_w_uk, mla_w_uv, mla_w_out,
              ffn_w_gate, ffn_w_up, ffn_w_down, final_norm_g):
    b_p, l_p, _ = x_prompt.shape
    b_s, l_s, _ = x_sample.shape
    past_len = page_table.shape[1] * PAGE_SIZE
    pos_p = jnp.arange(l_p, dtype=jnp.int32)
    pos_s = past_len + jnp.arange(l_s, dtype=jnp.int32)
    past_pos = jnp.arange(past_len, dtype=jnp.int32)

    yp, ys = x_prompt, x_sample
    gla_sp, gla_ss, mla_cp, mla_krp, mla_cs, mla_krs = [], [], [], [], [], []
    for i in range(DEPTH):
        j = i // N_MIXERS
        hp = rmsnorm(yp, ln_mixer_g[i])
        hs = rmsnorm(ys, ln_mixer_g[i])
        if i % N_MIXERS == 0:
            w = (gla_w_in[j], gla_w_gate_up[j], gla_b_gate[j], gla_norm_g[j], gla_w_out[j])
            s0_p = jnp.zeros((b_p, GLA_HEADS, GLA_DK, GLA_DV), hp.dtype)
            op, sp = gla_mixer(hp, s0_p, *w)
            os_, ss = gla_mixer(hs, state_gla[j], *w)
            gla_sp.append(sp)
            gla_ss.append(ss)
        else:
            w = (mla_w_in[j], mla_q_norm_g[j], mla_w_qb[j], mla_kv_norm_g[j],
                 mla_w_uk[j], mla_w_uv[j], mla_w_out[j])
            past_c = cache_mla_latent[j][page_table].reshape(b_s, past_len, MLA_KV_LORA)
            past_kr = cache_mla_krope[j][page_table].reshape(b_s, past_len, MLA_ROPE)
            op, cp, krp = mla_mixer(hp, pos_p, None, None, None, *w)
            os_, cs, krs = mla_mixer(hs, pos_s, past_c, past_kr, past_pos, *w)
            mla_cp.append(cp)
            mla_krp.append(krp)
            mla_cs.append(cs)
            mla_krs.append(krs)
        yp = yp + op
        ys = ys + os_
        yp = yp + swiglu(rmsnorm(yp, ln_ffn_g[i]), ffn_w_gate[i], ffn_w_up[i], ffn_w_down[i])
        ys = ys + swiglu(rmsnorm(ys, ln_ffn_g[i]), ffn_w_gate[i], ffn_w_up[i], ffn_w_down[i])

    y_prompt = rmsnorm(yp, final_norm_g)
    y_sample = rmsnorm(ys, final_norm_g)
    return (y_prompt, y_sample, jnp.stack(gla_sp), jnp.stack(gla_ss),
            jnp.stack(mla_cp), jnp.stack(mla_krp), jnp.stack(mla_cs), jnp.stack(mla_krs))
```

```python
import functools

import jax
import jax.numpy as jnp
from jax import lax
from jax.experimental import pallas as pl
from jax.experimental.pallas import tpu as pltpu

F32 = jnp.float32
BF16 = jnp.bfloat16

RMS_EPS = 1e-6
ROPE_THETA = 10000.0
GLA_GATE_TEMP = 16.0
GLA_CHUNK = 64
GLA_SUB = 16
EXP_CLAMP = 80.0
NEG_BIG = -1e30

V7X_VMEM_BYTES = 64 * 1024 * 1024
VMEM_LIMIT_BYTES = V7X_VMEM_BYTES - 8 * 1024 * 1024


def _cparams(*sem):
    return pltpu.CompilerParams(dimension_semantics=sem, vmem_limit_bytes=VMEM_LIMIT_BYTES)


def _rms(x, g):
    ms = jnp.mean(x * x, axis=-1, keepdims=True)
    return x * lax.rsqrt(ms + RMS_EPS) * g


def _silu(x):
    return x * jax.nn.sigmoid(x)


def _dot(a, b):
    return jnp.dot(a, b, preferred_element_type=F32)


def _dot_nt(a, b):
    return lax.dot_general(a, b, (((1,), (1,)), ((), ())), preferred_element_type=F32)


def _dot_tn(a, b):
    return lax.dot_general(a, b, (((0,), (0,)), ((), ())), preferred_element_type=F32)


def _norm_matmul_kernel(x_ref, g_ref, w_ref, o_ref, h_ref):
    @pl.when(pl.program_id(1) == 0)
    def _():
        h_ref[...] = _rms(x_ref[...], g_ref[...]).astype(BF16)

    o_ref[...] = _dot(h_ref[...], w_ref[...]).astype(o_ref.dtype)


def norm_matmul(x, g, w, out_dtype, *, tm=512, tn=1024):
    m, d = x.shape
    n = w.shape[1]
    tm, tn = min(tm, m), min(tn, n)
    return pl.pallas_call(
        _norm_matmul_kernel,
        grid=(m // tm, n // tn),
        in_specs=[
            pl.BlockSpec((tm, d), lambda i, j: (i, 0)),
            pl.BlockSpec((1, d), lambda i, j: (0, 0)),
            pl.BlockSpec((d, tn), lambda i, j: (0, j)),
        ],
        out_specs=pl.BlockSpec((tm, tn), lambda i, j: (i, j)),
        out_shape=jax.ShapeDtypeStruct((m, n), out_dtype),
        scratch_shapes=[pltpu.VMEM((tm, d), BF16)],
        compiler_params=_cparams("parallel", "arbitrary"),
        name="norm_matmul",
    )(x, g, w)


def _matmul_res_kernel(a_ref, w_ref, r_ref, o_ref):
    o_ref[...] = r_ref[...] + _dot(a_ref[...], w_ref[...])


def matmul_res(a, w, res, *, tm=512, tn=1024):
    m, k = a.shape
    n = w.shape[1]
    tm, tn = min(tm, m), min(tn, n)
    return pl.pallas_call(
        _matmul_res_kernel,
        grid=(m // tm, n // tn),
        in_specs=[
            pl.BlockSpec((tm, k), lambda i, j: (i, 0)),
            pl.BlockSpec((k, tn), lambda i, j: (0, j)),
            pl.BlockSpec((tm, tn), lambda i, j: (i, j)),
        ],
        out_specs=pl.BlockSpec((tm, tn), lambda i, j: (i, j)),
        out_shape=jax.ShapeDtypeStruct((m, n), F32),
        compiler_params=_cparams("parallel", "arbitrary"),
        name="matmul_res",
    )(a, w, res)


def _ffn_kernel(x_ref, g_ref, wg_ref, wu_ref, wd_ref, o_ref, h_ref):
    @pl.when(pl.program_id(1) == 0)
    def _():
        x = x_ref[...]
        h_ref[...] = _rms(x, g_ref[...]).astype(BF16)
        o_ref[...] = x

    h = h_ref[...]
    a = (_silu(_dot(h, wg_ref[...])) * _dot(h, wu_ref[...])).astype(BF16)
    o_ref[...] += _dot(a, wd_ref[...])


def ffn(x, g, wg, wu, wd, *, tm=512, tf=512):
    m, d = x.shape
    f = wg.shape[1]
    tm, tf = min(tm, m), min(tf, f)
    return pl.pallas_call(
        _ffn_kernel,
        grid=(m // tm, f // tf),
        in_specs=[
            pl.BlockSpec((tm, d), lambda i, j: (i, 0)),
            pl.BlockSpec((1, d), lambda i, j: (0, 0)),
            pl.BlockSpec((d, tf), lambda i, j: (0, j)),
            pl.BlockSpec((d, tf), lambda i, j: (0, j)),
            pl.BlockSpec((tf, d), lambda i, j: (j, 0)),
        ],
        out_specs=pl.BlockSpec((tm, d), lambda i, j: (i, 0)),
        out_shape=jax.ShapeDtypeStruct((m, d), F32),
        scratch_shapes=[pltpu.VMEM((tm, d), BF16)],
        compiler_params=_cparams("parallel", "arbitrary"),
        name="ffn",
    )(x, g, wg, wu, wd)


def _rmsnorm_kernel(x_ref, g_ref, o_ref):
    o_ref[...] = _rms(x_ref[...], g_ref[...])


def rmsnorm_rows(x, g, row0, nrows, *, tm=512):
    d = x.shape[1]
    tm = min(tm, nrows)
    blk0 = row0 // tm
    return pl.pallas_call(
        _rmsnorm_kernel,
        grid=(nrows // tm,),
        in_specs=[
            pl.BlockSpec((tm, d), lambda i: (blk0 + i, 0)),
            pl.BlockSpec((1, d), lambda i: (0, 0)),
        ],
        out_specs=pl.BlockSpec((tm, d), lambda i: (i, 0)),
        out_shape=jax.ShapeDtypeStruct((nrows, d), F32),
        compiler_params=_cparams("parallel"),
        name="final_rmsnorm",
    )(x, g)


def _gla_gate_kernel(x_ref, g_ref, wa_ref, wg_ref, bg_ref, o_ref):
    h = _rms(x_ref[...], g_ref[...]).astype(BF16)
    a = _dot(h, wa_ref[...]).astype(BF16)
    z = _dot(a, wg_ref[...]) + bg_ref[...]
    log_sig = jnp.minimum(z, 0.0) - jnp.log1p(jnp.exp(-jnp.abs(z)))
    o_ref[...] = log_sig * (1.0 / GLA_GATE_TEMP)


def gla_gate(x, g, wa, wg, bg, *, tm=512):
    m, d = x.shape
    n = wg.shape[1]
    tm = min(tm, m)
    return pl.pallas_call(
        _gla_gate_kernel,
        grid=(m // tm,),
        in_specs=[
            pl.BlockSpec((tm, d), lambda i: (i, 0)),
            pl.BlockSpec((1, d), lambda i: (0, 0)),
            pl.BlockSpec(wa.shape, lambda i: (0, 0)),
            pl.BlockSpec(wg.shape, lambda i: (0, 0)),
            pl.BlockSpec((1, n), lambda i: (0, 0)),
        ],
        out_specs=pl.BlockSpec((tm, n), lambda i: (i, 0)),
        out_shape=jax.ShapeDtypeStruct((m, n), F32),
        compiler_params=_cparams("parallel"),
        name="gla_gate",
    )(x, g, wa, wg, bg)


def _cumsum_rows(la):
    c = la.shape[0]
    if c <= 8:
        rows = lax.broadcasted_iota(jnp.int32, la.shape, 0)
        out = jnp.zeros_like(la)
        for j in range(c):
            out = out + jnp.where(rows >= j, la[j:j + 1, :], 0.0)
        return out
    r = lax.broadcasted_iota(jnp.int32, (c, c), 0)
    s = lax.broadcasted_iota(jnp.int32, (c, c), 1)
    tri = jnp.where(s <= r, 1.0, 0.0).astype(BF16)
    hi = la.astype(BF16)
    lo = (la - hi.astype(F32)).astype(BF16)
    return _dot(tri, hi) + _dot(tri, lo)


def _gla_chunk(qs, k, v, la, s, sub):
    c, dk = qs.shape
    b = _cumsum_rows(la)
    b_last = b[c - 1:c, :]
    o = _dot((qs * jnp.exp(b)).astype(BF16), s.astype(BF16))
    col = lax.broadcasted_iota(jnp.int32, (sub, c), 1)
    row = lax.broadcasted_iota(jnp.int32, (sub, c), 0)
    parts = []
    for i in range(c // sub):
        lo = i * sub
        ref = b[lo - 1:lo, :] if i else jnp.zeros((1, dk), F32)
        qi = (qs[lo:lo + sub, :] * jnp.exp(b[lo:lo + sub, :] - ref)).astype(BF16)
        kd = (k * jnp.exp(jnp.minimum(ref - b, EXP_CLAMP))).astype(BF16)
        parts.append(jnp.where(col <= row + lo, _dot_nt(qi, kd), 0.0))
    att = parts[0] if len(parts) == 1 else jnp.concatenate(parts, axis=0)
    o = o + _dot(att.astype(BF16), v)
    kdl = (k * jnp.exp(b_last - b)).astype(BF16)
    decay_col = jnp.exp(b[c - 8:c, :].T[:, 7:8])
    s_new = decay_col * s + _dot_tn(kdl, v)
    return o, s_new


def _gla_gate_out(o, r, gn):
    return (_rms(o, gn) * _silu(r.astype(F32))).astype(BF16)


def _gla_prompt_kernel(q_ref, k_ref, v_ref, r_ref, la_ref, gn_ref, o_ref, s_ref, *, chunk, sub, scale):
    @pl.when(pl.program_id(2) == 0)
    def _():
        s_ref[...] = jnp.zeros_like(s_ref)

    def body(c, carry):
        rows = pl.ds(pl.multiple_of(c * chunk, chunk), chunk)
        o, s_new = _gla_chunk(q_ref[rows, :] * scale, k_ref[rows, :], v_ref[rows, :],
                              la_ref[rows, :], s_ref[...], sub)
        s_ref[...] = s_new
        o_ref[rows, :] = _gla_gate_out(o, r_ref[rows, :], gn_ref[...])
        return carry

    lax.fori_loop(0, q_ref.shape[0] // chunk, body, 0)


def gla_prompt(qk, vr, la, gn, s_all, layer, *, nb, seq, heads, dk, dv, n_layers, tb=512):
    m = qk.shape[0]
    chunk = min(GLA_CHUNK, seq)
    tb = min(tb, seq)
    nt = seq // tb
    row = lambda b, h, t: b * nt + t
    in_specs = [
        pl.BlockSpec((tb, dk), lambda b, h, t: (row(b, h, t), h)),
        pl.BlockSpec((tb, dk), lambda b, h, t: (row(b, h, t), heads + h)),
        pl.BlockSpec((tb, dv), lambda b, h, t: (row(b, h, t), h)),
        pl.BlockSpec((tb, dv), lambda b, h, t: (row(b, h, t), heads + h)),
        pl.BlockSpec((tb, dk), lambda b, h, t: (row(b, h, t), h)),
        pl.BlockSpec((1, dv), lambda b, h, t: (0, 0)),
    ]
    args = [qk, qk, vr, vr, la, gn]
    aliases = {}
    if s_all is not None:
        in_specs.append(pl.BlockSpec(memory_space=pl.ANY))
        args.append(s_all)
        aliases = {len(args) - 1: 1}
    kern = functools.partial(_gla_prompt_kernel, chunk=chunk, sub=min(GLA_SUB, chunk), scale=dk ** -0.5)
    if s_all is not None:
        kern = functools.partial(_drop_arg, kern, 6)
    return pl.pallas_call(
        kern,
        grid=(nb, heads, nt),
        in_specs=in_specs,
        out_specs=[
            pl.BlockSpec((tb, dv), lambda b, h, t: (row(b, h, t), h)),
            pl.BlockSpec((None, None, None, dk, dv), lambda b, h, t: (layer, b, h, 0, 0)),
        ],
        out_shape=[
            jax.ShapeDtypeStruct((m, heads * dv), BF16),
            jax.ShapeDtypeStruct((n_layers, nb, heads, dk, dv), F32),
        ],
        input_output_aliases=aliases,
        compiler_params=_cparams("parallel", "parallel", "arbitrary"),
        name="gla_prompt",
    )(*args)


def _drop_arg(fn, idx, *refs):
    return fn(*refs[:idx], *refs[idx + 1:])


def _drop_args(fn, idxs, *refs):
    return fn(*[r for i, r in enumerate(refs) if i not in idxs])


def _gla_sample_kernel(q_ref, k_ref, v_ref, r_ref, la_ref, gn_ref, s0_ref, o_ref, s_ref,
                       *, seq, gb, heads, dk, dv, scale):
    for g in range(gb):
        rows = slice(g * seq, (g + 1) * seq)
        for h in range(heads):
            kc = slice(h * dk, (h + 1) * dk)
            vc = slice(h * dv, (h + 1) * dv)
            o, s_new = _gla_chunk(q_ref[rows, kc] * scale, k_ref[rows, kc], v_ref[rows, vc],
                                  la_ref[rows, kc], s0_ref[g, h], seq)
            s_ref[g, h] = s_new
            o_ref[rows, vc] = _gla_gate_out(o, r_ref[rows, vc], gn_ref[...])


def gla_sample(qk, vr, la, gn, state, og, s_all, layer, *, row0, nb, seq, heads, dk, dv, gb=2):
    m = qk.shape[0]
    n_layers = state.shape[0]
    tb = gb * seq
    blk0 = row0 // tb
    in_specs = [
        pl.BlockSpec((tb, heads * dk), lambda i: (blk0 + i, 0)),
        pl.BlockSpec((tb, heads * dk), lambda i: (blk0 + i, 1)),
        pl.BlockSpec((tb, heads * dv), lambda i: (blk0 + i, 0)),
        pl.BlockSpec((tb, heads * dv), lambda i: (blk0 + i, 1)),
        pl.BlockSpec((tb, heads * dk), lambda i: (blk0 + i, 0)),
        pl.BlockSpec((1, dv), lambda i: (0, 0)),
        pl.BlockSpec((None, gb, heads, dk, dv), lambda i: (layer, i, 0, 0, 0)),
        pl.BlockSpec(memory_space=pl.ANY),
    ]
    args = [qk, qk, vr, vr, la, gn, state, og]
    aliases = {7: 0}
    drop = [7]
    if s_all is not None:
        in_specs.append(pl.BlockSpec(memory_space=pl.ANY))
        args.append(s_all)
        aliases[8] = 1
        drop.append(8)
    kern = functools.partial(_gla_sample_kernel, seq=seq, gb=gb, heads=heads, dk=dk, dv=dv,
                             scale=dk ** -0.5)
    kern = functools.partial(_drop_args, kern, tuple(drop))
    return pl.pallas_call(
        kern,
        grid=(nb // gb,),
        in_specs=in_specs,
        out_specs=[
            pl.BlockSpec((tb, heads * dv), lambda i: (blk0 + i, 0)),
            pl.BlockSpec((None, gb, heads, dk, dv), lambda i: (layer, i, 0, 0, 0)),
        ],
        out_shape=[
            jax.ShapeDtypeStruct((m, heads * dv), BF16),
            jax.ShapeDtypeStruct((n_layers, nb, heads, dk, dv), F32),
        ],
        input_output_aliases=aliases,
        compiler_params=_cparams("parallel"),
        name="gla_sample",
    )(*args)


def _mla_pre_kernel(x_ref, g_ref, win_ref, gq_ref, gkv_ref, wqb_ref, wuk_ref, cos_ref, sin_ref,
                    c_ref, kr_ref, cb_ref, krb_ref, ql_ref, qr_ref,
                    *, heads, q_lora, kv_lora, nope, rope, scale):
    h = _rms(x_ref[...], g_ref[...]).astype(BF16)
    p = _dot(h, win_ref[...])
    cos = cos_ref[...]
    sin = sin_ref[...]
    c = _rms(p[:, q_lora:q_lora + kv_lora], gkv_ref[...])
    c_ref[...] = c
    cb_ref[...] = c.astype(BF16)
    o = q_lora + kv_lora
    kr = p[:, o:o + rope] * cos[:, :rope] + p[:, o + rope:o + 2 * rope] * sin[:, :rope]
    kr_ref[...] = kr
    krb_ref[...] = kr.astype(BF16)
    cqn = _rms(p[:, :q_lora], gq_ref[...]).astype(BF16)
    q = _dot(cqn, wqb_ref[...])
    n0 = heads * nope
    n1 = n0 + heads * rope
    pair = 2 * rope
    for hp in range(heads * rope // pair):
        a = q[:, n0 + hp * pair:n0 + (hp + 1) * pair]
        b = q[:, n1 + hp * pair:n1 + (hp + 1) * pair]
        qr_ref[:, hp * pair:(hp + 1) * pair] = ((a * cos + b * sin) * scale).astype(BF16)
    for hd in range(heads):
        qn = q[:, hd * nope:(hd + 1) * nope].astype(BF16)
        ql_ref[:, hd * kv_lora:(hd + 1) * kv_lora] = (_dot(qn, wuk_ref[hd]) * scale).astype(BF16)


def mla_pre(x, g, win_ext, gq, gkv, wqb_ext, wuk_t, cos, sin, *, heads, q_lora, kv_lora, nope, rope,
            tm=256):
    m, d = x.shape
    tm = min(tm, m)
    full = lambda a: pl.BlockSpec(a.shape, lambda i: (0,) * a.ndim)
    rows = lambda n: pl.BlockSpec((tm, n), lambda i: (i, 0))
    kern = functools.partial(_mla_pre_kernel, heads=heads, q_lora=q_lora, kv_lora=kv_lora, nope=nope,
                             rope=rope, scale=(nope + rope) ** -0.5)
    return pl.pallas_call(
        kern,
        grid=(m // tm,),
        in_specs=[rows(d), full(g), full(win_ext), full(gq), full(gkv), full(wqb_ext), full(wuk_t),
                  rows(cos.shape[1]), rows(sin.shape[1])],
        out_specs=[rows(kv_lora), rows(rope), rows(kv_lora), rows(rope), rows(heads * kv_lora),
                   rows(heads * rope)],
        out_shape=[
            jax.ShapeDtypeStruct((m, kv_lora), F32),
            jax.ShapeDtypeStruct((m, rope), F32),
            jax.ShapeDtypeStruct((m, kv_lora), BF16),
            jax.ShapeDtypeStruct((m, rope), BF16),
            jax.ShapeDtypeStruct((m, heads * kv_lora), BF16),
            jax.ShapeDtypeStruct((m, heads * rope), BF16),
        ],
        compiler_params=_cparams("parallel"),
        name="mla_pre",
    )(x, g, win_ext, gq, gkv, wqb_ext, wuk_t, cos, sin)


def _row_token(shape, heads):
    assert heads & (heads - 1) == 0
    return lax.broadcasted_iota(jnp.int32, shape, 0) >> (heads.bit_length() - 1)


def _flash_update(s, v, m_ref, l_ref, acc_ref):
    m_prev = m_ref[...]
    m_new = jnp.maximum(m_prev, jnp.max(s, axis=-1, keepdims=True))
    alpha = jnp.exp(m_prev - m_new)
    p = jnp.exp(s - m_new)
    l_ref[...] = alpha * l_ref[...] + jnp.sum(p, axis=-1, keepdims=True)
    acc_ref[...] = alpha * acc_ref[...] + _dot(p.astype(BF16), v)
    m_ref[...] = m_new


def _flash_init(m_ref, l_ref, acc_ref):
    m_ref[...] = jnp.full_like(m_ref, NEG_BIG)
    l_ref[...] = jnp.zeros_like(l_ref)
    acc_ref[...] = jnp.zeros_like(acc_ref)


def _mla_attn_prompt_kernel(q_ref, qr_ref, c_ref, kr_ref, o_ref, m_ref, l_ref, acc_ref,
                            *, qb, tk, heads):
    qi = pl.program_id(1)
    ki = pl.program_id(2)

    @pl.when(ki == 0)
    def _():
        _flash_init(m_ref, l_ref, acc_ref)

    @pl.when(ki * tk <= qi * qb + qb - 1)
    def _():
        c = c_ref[...]
        s = _dot_nt(q_ref[...], c) + _dot_nt(qr_ref[...], kr_ref[...])
        qpos = qi * qb + _row_token(s.shape, heads)
        kpos = ki * tk + lax.broadcasted_iota(jnp.int32, s.shape, 1)
        _flash_update(jnp.where(kpos <= qpos, s, NEG_BIG), c, m_ref, l_ref, acc_ref)

    @pl.when(ki == pl.num_programs(2) - 1)
    def _():
        o_ref[...] = (acc_ref[...] / l_ref[...]).astype(o_ref.dtype)


def mla_attn_prompt(ql, qr, cb, krb, *, nb, seq, heads, kv_lora, rope, qb=64, tk=256):
    mh = ql.shape[0]
    qb, tk = min(qb, seq), min(tk, seq)
    r = qb * heads
    nq, nk = seq // qb, seq // tk

    def kblk(b, qi, ki):
        return b * nk + jnp.minimum(ki, (qi * qb + qb - 1) // tk)

    kern = functools.partial(_mla_attn_prompt_kernel, qb=qb, tk=tk, heads=heads)
    return pl.pallas_call(
        kern,
        grid=(nb, nq, nk),
        in_specs=[
            pl.BlockSpec((r, kv_lora), lambda b, qi, ki: (b * nq + qi, 0)),
            pl.BlockSpec((r, rope), lambda b, qi, ki: (b * nq + qi, 0)),
            pl.BlockSpec((tk, kv_lora), lambda b, qi, ki: (kblk(b, qi, ki), 0)),
            pl.BlockSpec((tk, rope), lambda b, qi, ki: (kblk(b, qi, ki), 0)),
        ],
        out_specs=pl.BlockSpec((r, kv_lora), lambda b, qi, ki: (b * nq + qi, 0)),
        out_shape=jax.ShapeDtypeStruct((mh, kv_lora), BF16),
        scratch_shapes=[pltpu.VMEM((r, 1), F32), pltpu.VMEM((r, 1), F32), pltpu.VMEM((r, kv_lora), F32)],
        compiler_params=_cparams("parallel", "parallel", "arbitrary"),
        name="mla_attn_prompt",
    )(ql, qr, cb, krb)


def _mla_attn_sample_kernel(pt_ref, q_ref, qr_ref, cn_ref, krn_ref, *rest, pp, page, heads):
    del pt_ref
    lat_refs = rest[:pp]
    kro_refs = rest[pp:2 * pp]
    o_ref = rest[2 * pp + 1]
    m_ref, l_ref, acc_ref, kbuf_ref, krbuf_ref = rest[2 * pp + 2:]
    g = pl.program_id(1)

    @pl.when(g == 0)
    def _():
        _flash_init(m_ref, l_ref, acc_ref)

    for i in range(pp):
        kbuf_ref[i * page:(i + 1) * page, :] = lat_refs[i][...].astype(BF16)
        krbuf_ref[i * page:(i + 1) * page, :] = kro_refs[i][...].astype(BF16)
    q = q_ref[...]
    qr = qr_ref[...]
    kb = kbuf_ref[...]
    _flash_update(_dot_nt(q, kb) + _dot_nt(qr, krbuf_ref[...]), kb, m_ref, l_ref, acc_ref)

    @pl.when(g == pl.num_programs(1) - 1)
    def _():
        cn = cn_ref[...].astype(BF16)
        s = _dot_nt(q, cn) + _dot_nt(qr, krn_ref[...].astype(BF16))
        qtok = _row_token(s.shape, heads)
        ktok = lax.broadcasted_iota(jnp.int32, s.shape, 1)
        _flash_update(jnp.where(ktok <= qtok, s, NEG_BIG), cn, m_ref, l_ref, acc_ref)
        o_ref[...] = (acc_ref[...] / l_ref[...]).astype(o_ref.dtype)


def mla_attn_sample(page_table, ql, qr, c_new, kr_new, cache_lat, cache_kr, o_lat, layer,
                    *, row0, nb, seq, heads, kv_lora, rope, pp=8):
    n_pages = page_table.shape[1]
    page = cache_lat.shape[2]
    r = seq * heads
    qblk0 = row0 * heads // r
    nblk0 = row0 // seq
    kern = functools.partial(_mla_attn_sample_kernel, pp=pp, page=page, heads=heads)
    lat_specs = [
        pl.BlockSpec((None, None, page, kv_lora),
                     functools.partial(lambda i, b, g, pt: (layer, pt[b, g * pp + i], 0, 0), i))
        for i in range(pp)
    ]
    kr_specs = [
        pl.BlockSpec((None, None, page, rope),
                     functools.partial(lambda i, b, g, pt: (layer, pt[b, g * pp + i], 0, 0), i))
        for i in range(pp)
    ]
    grid_spec = pltpu.PrefetchScalarGridSpec(
        num_scalar_prefetch=1,
        grid=(nb, n_pages // pp),
        in_specs=[
            pl.BlockSpec((r, kv_lora), lambda b, g, pt: (qblk0 + b, 0)),
            pl.BlockSpec((r, rope), lambda b, g, pt: (qblk0 + b, 0)),
            pl.BlockSpec((seq, kv_lora), lambda b, g, pt: (nblk0 + b, 0)),
            pl.BlockSpec((seq, rope), lambda b, g, pt: (nblk0 + b, 0)),
            *lat_specs, *kr_specs,
            pl.BlockSpec(memory_space=pl.ANY),
        ],
        out_specs=pl.BlockSpec((r, kv_lora), lambda b, g, pt: (qblk0 + b, 0)),
        scratch_shapes=[
            pltpu.VMEM((r, 1), F32), pltpu.VMEM((r, 1), F32), pltpu.VMEM((r, kv_lora), F32),
            pltpu.VMEM((pp * page, kv_lora), BF16), pltpu.VMEM((pp * page, rope), BF16),
        ],
    )
    return pl.pallas_call(
        kern,
        grid_spec=grid_spec,
        out_shape=jax.ShapeDtypeStruct(o_lat.shape, o_lat.dtype),
        input_output_aliases={5 + 2 * pp: 0},
        compiler_params=_cparams("parallel", "arbitrary"),
        name="mla_attn_sample",
    )(page_table, ql, qr, c_new, kr_new, *([cache_lat] * pp), *([cache_kr] * pp), o_lat)


def _mla_post_kernel(ol_ref, wuv_ref, wo_ref, x_ref, o_ref, ob_ref, *, heads, kv_lora, vdim):
    @pl.when(pl.program_id(1) == 0)
    def _():
        for hd in range(heads):
            ob_ref[:, hd * vdim:(hd + 1) * vdim] = _dot(
                ol_ref[:, hd * kv_lora:(hd + 1) * kv_lora], wuv_ref[hd]).astype(BF16)

    o_ref[...] = x_ref[...] + _dot(ob_ref[...], wo_ref[...])


def mla_post(o_lat, wuv, wo, x, *, heads, kv_lora, vdim, tm=256, tn=1024):
    m, d = x.shape
    tm, tn = min(tm, m), min(tn, d)
    kern = functools.partial(_mla_post_kernel, heads=heads, kv_lora=kv_lora, vdim=vdim)
    return pl.pallas_call(
        kern,
        grid=(m // tm, d // tn),
        in_specs=[
            pl.BlockSpec((tm, heads * kv_lora), lambda i, j: (i, 0)),
            pl.BlockSpec(wuv.shape, lambda i, j: (0, 0, 0)),
            pl.BlockSpec((heads * vdim, tn), lambda i, j: (0, j)),
            pl.BlockSpec((tm, tn), lambda i, j: (i, j)),
        ],
        out_specs=pl.BlockSpec((tm, tn), lambda i, j: (i, j)),
        out_shape=jax.ShapeDtypeStruct((m, d), F32),
        scratch_shapes=[pltpu.VMEM((tm, heads * vdim), BF16)],
        compiler_params=_cparams("parallel", "arbitrary"),
        name="mla_post",
    )(o_lat, wuv, wo, x)


def _rot_half_cols(w, half):
    return jnp.concatenate([-w[..., half:], w[..., :half]], axis=-1)


def gla_layer(x, ln_g, w_in, w_gate_up, b_gate, g_norm, w_out, state_gla, states, layer, dims):
    nbp, lp, nbs, ls, heads, dk, dv, rank = dims
    nq, nv = heads * dk, heads * dv
    mp = nbp * lp
    g = ln_g.reshape(1, -1)
    w_qk = w_in[:, :2 * nq].astype(BF16)
    w_vr = w_in[:, 2 * nq:2 * nq + 2 * nv].astype(BF16)
    lane = 128
    wa = jnp.pad(w_in[:, 2 * nq + 2 * nv:], ((0, 0), (0, lane - rank))).astype(BF16)
    wg = jnp.pad(w_gate_up, ((0, lane - rank), (0, 0))).astype(BF16)
    qk = norm_matmul(x, g, w_qk, F32)
    vr = norm_matmul(x, g, w_vr, BF16)
    la = gla_gate(x, g, wa, wg, b_gate.reshape(1, -1))
    gn = g_norm.reshape(1, -1)
    sp_all, ss_all = states
    og, sp_all = gla_prompt(qk, vr, la, gn, sp_all, layer, nb=nbp, seq=lp, heads=heads, dk=dk, dv=dv,
                            n_layers=state_gla.shape[0])
    og, ss_all = gla_sample(qk, vr, la, gn, state_gla, og, ss_all, layer, row0=mp, nb=nbs, seq=ls,
                            heads=heads, dk=dk, dv=dv)
    x = matmul_res(og, w_out.astype(BF16), x)
    return x, (sp_all, ss_all)


def mla_layer(x, ln_g, w_in, g_q, w_qb, g_kv, w_uk, w_uv, w_out, cache_lat, cache_kr, page_table,
              cos, sin, layer, dims):
    nbp, lp, nbs, ls, heads, q_lora, kv_lora, nope, rope, vdim = dims
    m = x.shape[0]
    mp = nbp * lp
    half = rope // 2
    o = q_lora + kv_lora
    win_ext = jnp.concatenate([w_in, _rot_half_cols(w_in[:, o:o + rope], half)], axis=1).astype(BF16)
    wqb = w_qb.reshape(q_lora, heads, nope + rope)
    w_rope = wqb[:, :, nope:]
    wqb_ext = jnp.concatenate([
        wqb[:, :, :nope].reshape(q_lora, heads * nope),
        w_rope.reshape(q_lora, heads * rope),
        _rot_half_cols(w_rope, half).reshape(q_lora, heads * rope)], axis=1).astype(BF16)
    wuk_t = jnp.transpose(w_uk, (1, 2, 0)).astype(BF16)
    wuv = jnp.transpose(w_uv, (1, 0, 2)).astype(BF16)
    c_new, kr_new, cb, krb, ql, qr = mla_pre(
        x, ln_g.reshape(1, -1), win_ext, g_q.reshape(1, -1), g_kv.reshape(1, -1), wqb_ext, wuk_t,
        cos, sin, heads=heads, q_lora=q_lora, kv_lora=kv_lora, nope=nope, rope=rope)
    ql = ql.reshape(m * heads, kv_lora)
    qr = qr.reshape(m * heads, rope)
    o_lat = mla_attn_prompt(ql, qr, cb, krb, nb=nbp, seq=lp, heads=heads, kv_lora=kv_lora, rope=rope)
    o_lat = mla_attn_sample(page_table, ql, qr, c_new, kr_new, cache_lat, cache_kr, o_lat, layer,
                            row0=mp, nb=nbs, seq=ls, heads=heads, kv_lora=kv_lora, rope=rope)
    x = mla_post(o_lat.reshape(m, heads * kv_lora), wuv, w_out.astype(BF16), x,
                 heads=heads, kv_lora=kv_lora, vdim=vdim)
    return x, c_new, kr_new


def kernel(x_prompt, x_sample, state_gla, cache_mla_latent, cache_mla_krope, page_table, ln_mixer_g, ln_ffn_g, gla_w_in, gla_w_gate_up, gla_b_gate, gla_norm_g, gla_w_out, mla_w_in, mla_q_norm_g, mla_w_qb, mla_kv_norm_g, mla_w_uk, mla_w_uv, mla_w_out, ffn_w_gate, ffn_w_up, ffn_w_down, final_norm_g):
    nbp, lp, d = x_prompt.shape
    nbs, ls, _ = x_sample.shape
    mp, ms = nbp * lp, nbs * ls
    depth = ln_mixer_g.shape[0]
    heads_g, dk, dv = state_gla.shape[2:]
    rank = gla_w_gate_up.shape[1]
    kv_lora, heads_m, nope = mla_w_uk.shape[1:]
    vdim = mla_w_uv.shape[3]
    q_lora = mla_q_norm_g.shape[1]
    rope = cache_mla_krope.shape[3]
    past_len = page_table.shape[1] * cache_mla_latent.shape[2]

    x = jnp.concatenate([x_prompt.reshape(mp, d), x_sample.reshape(ms, d)], axis=0)

    half = rope // 2
    inv = 1.0 / (ROPE_THETA ** (jnp.arange(half, dtype=F32) / half))
    pos = jnp.concatenate([jnp.tile(jnp.arange(lp, dtype=jnp.int32), nbp),
                           jnp.tile(past_len + jnp.arange(ls, dtype=jnp.int32), nbs)])
    ang = pos.astype(F32)[:, None] * inv[None, :]
    cos = jnp.tile(jnp.cos(ang), (1, 2 * rope // half))
    sin = jnp.tile(jnp.sin(ang), (1, 2 * rope // half))

    gla_dims = (nbp, lp, nbs, ls, heads_g, dk, dv, rank)
    mla_dims = (nbp, lp, nbs, ls, heads_m, q_lora, kv_lora, nope, rope, vdim)
    gla_states = (None, None)
    mla_c, mla_kr = [], []
    for i in range(depth):
        j = i // 2
        if i % 2 == 0:
            x, gla_states = gla_layer(x, ln_mixer_g[i], gla_w_in[j], gla_w_gate_up[j], gla_b_gate[j],
                                      gla_norm_g[j], gla_w_out[j], state_gla, gla_states, j, gla_dims)
        else:
            x, c_new, kr_new = mla_layer(x, ln_mixer_g[i], mla_w_in[j], mla_q_norm_g[j], mla_w_qb[j],
                                         mla_kv_norm_g[j], mla_w_uk[j], mla_w_uv[j], mla_w_out[j],
                                         cache_mla_latent, cache_mla_krope, page_table, cos, sin, j,
                                         mla_dims)
            mla_c.append(c_new)
            mla_kr.append(kr_new)
        x = ffn(x, ln_ffn_g[i].reshape(1, -1), ffn_w_gate[i].astype(BF16), ffn_w_up[i].astype(BF16),
                ffn_w_down[i].astype(BF16))

    fg = final_norm_g.reshape(1, -1)
    y_prompt = rmsnorm_rows(x, fg, 0, mp).reshape(nbp, lp, d)
    y_sample = rmsnorm_rows(x, fg, mp, ms).reshape(nbs, ls, d)
    c_all = jnp.stack(mla_c)
    kr_all = jnp.stack(mla_kr)
    return (y_prompt, y_sample, gla_states[0], gla_states[1],
            c_all[:, :mp].reshape(-1, nbp, lp, kv_lora), kr_all[:, :mp].reshape(-1, nbp, lp, rope),
            c_all[:, mp:].reshape(-1, nbs, ls, kv_lora), kr_all[:, mp:].reshape(-1, nbs, ls, rope))
```

```python
import functools

import jax
import jax.numpy as jnp
from jax import lax
from jax.experimental import pallas as pl
from jax.experimental.pallas import tpu as pltpu

F32 = jnp.float32
BF16 = jnp.bfloat16

RMS_EPS = 1e-6
ROPE_THETA = 10000.0
GLA_GATE_TEMP = 16.0
GLA_CHUNK = 64
GLA_SUB = 16
EXP_CLAMP = 80.0
NEG_BIG = -1e30

V7X_VMEM_BYTES = 64 * 1024 * 1024
VMEM_LIMIT_BYTES = V7X_VMEM_BYTES - 8 * 1024 * 1024


def _cparams(*sem):
    return pltpu.CompilerParams(dimension_semantics=sem, vmem_limit_bytes=VMEM_LIMIT_BYTES)


def _rms(x, g):
    ms = jnp.mean(x * x, axis=-1, keepdims=True)
    return x * lax.rsqrt(ms + RMS_EPS) * g


def _silu(x):
    return x * jax.nn.sigmoid(x)


def _dot(a, b):
    return jnp.dot(a, b, preferred_element_type=F32)


def _dot_nt(a, b):
    return lax.dot_general(a, b, (((1,), (1,)), ((), ())), preferred_element_type=F32)


def _dot_tn(a, b):
    return lax.dot_general(a, b, (((0,), (0,)), ((), ())), preferred_element_type=F32)


def _norm_matmul_kernel(x_ref, g_ref, w_ref, o_ref, h_ref):
    @pl.when(pl.program_id(1) == 0)
    def _():
        h_ref[...] = _rms(x_ref[...], g_ref[...]).astype(BF16)

    o_ref[...] = _dot(h_ref[...], w_ref[...]).astype(o_ref.dtype)


def norm_matmul(x, g, w, out_dtype, *, tm=512, tn=1024):
    m, d = x.shape
    n = w.shape[1]
    tm, tn = min(tm, m), min(tn, n)
    return pl.pallas_call(
        _norm_matmul_kernel,
        grid=(m // tm, n // tn),
        in_specs=[
            pl.BlockSpec((tm, d), lambda i, j: (i, 0)),
            pl.BlockSpec((1, d), lambda i, j: (0, 0)),
            pl.BlockSpec((d, tn), lambda i, j: (0, j)),
        ],
        out_specs=pl.BlockSpec((tm, tn), lambda i, j: (i, j)),
        out_shape=jax.ShapeDtypeStruct((m, n), out_dtype),
        scratch_shapes=[pltpu.VMEM((tm, d), BF16)],
        compiler_params=_cparams("parallel", "arbitrary"),
        name="norm_matmul",
    )(x, g, w)


def _matmul_res_kernel(a_ref, w_ref, r_ref, o_ref):
    o_ref[...] = r_ref[...] + _dot(a_ref[...], w_ref[...])


def matmul_res(a, w, res, *, tm=512, tn=1024):
    m, k = a.shape
    n = w.shape[1]
    tm, tn = min(tm, m), min(tn, n)
    return pl.pallas_call(
        _matmul_res_kernel,
        grid=(m // tm, n // tn),
        in_specs=[
            pl.BlockSpec((tm, k), lambda i, j: (i, 0)),
            pl.BlockSpec((k, tn), lambda i, j: (0, j)),
            pl.BlockSpec((tm, tn), lambda i, j: (i, j)),
        ],
        out_specs=pl.BlockSpec((tm, tn), lambda i, j: (i, j)),
        out_shape=jax.ShapeDtypeStruct((m, n), F32),
        compiler_params=_cparams("parallel", "arbitrary"),
        name="matmul_res",
    )(a, w, res)


def _ffn_kernel(x_ref, g_ref, wg_ref, wu_ref, wd_ref, o_ref, h_ref):
    @pl.when(pl.program_id(1) == 0)
    def _():
        x = x_ref[...]
        h_ref[...] = _rms(x, g_ref[...]).astype(BF16)
        o_ref[...] = x

    h = h_ref[...]
    a = (_silu(_dot(h, wg_ref[...])) * _dot(h, wu_ref[...])).astype(BF16)
    o_ref[...] += _dot(a, wd_ref[...])


def ffn(x, g, wg, wu, wd, *, tm=512, tf=512):
    m, d = x.shape
    f = wg.shape[1]
    tm, tf = min(tm, m), min(tf, f)
    return pl.pallas_call(
        _ffn_kernel,
        grid=(m // tm, f // tf),
        in_specs=[
            pl.BlockSpec((tm, d), lambda i, j: (i, 0)),
            pl.BlockSpec((1, d), lambda i, j: (0, 0)),
            pl.BlockSpec((d, tf), lambda i, j: (0, j)),
            pl.BlockSpec((d, tf), lambda i, j: (0, j)),
            pl.BlockSpec((tf, d), lambda i, j: (j, 0)),
        ],
        out_specs=pl.BlockSpec((tm, d), lambda i, j: (i, 0)),
        out_shape=jax.ShapeDtypeStruct((m, d), F32),
        scratch_shapes=[pltpu.VMEM((tm, d), BF16)],
        compiler_params=_cparams("parallel", "arbitrary"),
        name="ffn",
    )(x, g, wg, wu, wd)


def _rmsnorm_kernel(x_ref, g_ref, o_ref):
    o_ref[...] = _rms(x_ref[...], g_ref[...])


def rmsnorm_rows(x, g, row0, nrows, *, tm=512):
    d = x.shape[1]
    tm = min(tm, nrows)
    blk0 = row0 // tm
    return pl.pallas_call(
        _rmsnorm_kernel,
        grid=(nrows // tm,),
        in_specs=[
            pl.BlockSpec((tm, d), lambda i: (blk0 + i, 0)),
            pl.BlockSpec((1, d), lambda i: (0, 0)),
        ],
        out_specs=pl.BlockSpec((tm, d), lambda i: (i, 0)),
        out_shape=jax.ShapeDtypeStruct((nrows, d), F32),
        compiler_params=_cparams("parallel"),
        name="final_rmsnorm",
    )(x, g)


def _gla_gate_kernel(x_ref, g_ref, wa_ref, wg_ref, bg_ref, o_ref):
    h = _rms(x_ref[...], g_ref[...]).astype(BF16)
    a = _dot(h, wa_ref[...]).astype(BF16)
    z = _dot(a, wg_ref[...]) + bg_ref[...]
    log_sig = jnp.minimum(z, 0.0) - jnp.log1p(jnp.exp(-jnp.abs(z)))
    o_ref[...] = log_sig * (1.0 / GLA_GATE_TEMP)


def gla_gate(x, g, wa, wg, bg, *, tm=512):
    m, d = x.shape
    n = wg.shape[1]
    tm = min(tm, m)
    return pl.pallas_call(
        _gla_gate_kernel,
        grid=(m // tm,),
        in_specs=[
            pl.BlockSpec((tm, d), lambda i: (i, 0)),
            pl.BlockSpec((1, d), lambda i: (0, 0)),
            pl.BlockSpec(wa.shape, lambda i: (0, 0)),
            pl.BlockSpec(wg.shape, lambda i: (0, 0)),
            pl.BlockSpec((1, n), lambda i: (0, 0)),
        ],
        out_specs=pl.BlockSpec((tm, n), lambda i: (i, 0)),
        out_shape=jax.ShapeDtypeStruct((m, n), F32),
        compiler_params=_cparams("parallel"),
        name="gla_gate",
    )(x, g, wa, wg, bg)


def _cumsum_rows(la):
    c = la.shape[0]
    if c <= 8:
        rows = lax.broadcasted_iota(jnp.int32, la.shape, 0)
        out = jnp.zeros_like(la)
        for j in range(c):
            out = out + jnp.where(rows >= j, la[j:j + 1, :], 0.0)
        return out
    r = lax.broadcasted_iota(jnp.int32, (c, c), 0)
    s = lax.broadcasted_iota(jnp.int32, (c, c), 1)
    tri = jnp.where(s <= r, 1.0, 0.0).astype(BF16)
    hi = la.astype(BF16)
    lo = (la - hi.astype(F32)).astype(BF16)
    return _dot(tri, hi) + _dot(tri, lo)


def _gla_chunk(qs, k, v, la, s, sub):
    c, dk = qs.shape
    b = _cumsum_rows(la)
    b_last = b[c - 1:c, :]
    o = _dot((qs * jnp.exp(b)).astype(BF16), s.astype(BF16))
    col = lax.broadcasted_iota(jnp.int32, (sub, c), 1)
    row = lax.broadcasted_iota(jnp.int32, (sub, c), 0)
    parts = []
    for i in range(c // sub):
        lo = i * sub
        ref = b[lo - 1:lo, :] if i else jnp.zeros((1, dk), F32)
        qi = (qs[lo:lo + sub, :] * jnp.exp(b[lo:lo + sub, :] - ref)).astype(BF16)
        kd = (k * jnp.exp(jnp.minimum(ref - b, EXP_CLAMP))).astype(BF16)
        parts.append(jnp.where(col <= row + lo, _dot_nt(qi, kd), 0.0))
    att = parts[0] if len(parts) == 1 else jnp.concatenate(parts, axis=0)
    o = o + _dot(att.astype(BF16), v)
    kdl = (k * jnp.exp(b_last - b)).astype(BF16)
    decay_col = jnp.exp(b[c - 8:c, :].T[:, 7:8])
    s_new = decay_col * s + _dot_tn(kdl, v)
    return o, s_new


def _gla_gate_out(o, r, gn):
    return (_rms(o, gn) * _silu(r.astype(F32))).astype(BF16)


def _gla_prompt_kernel(q_ref, k_ref, v_ref, r_ref, la_ref, gn_ref, o_ref, s_ref, *, chunk, sub, scale):
    @pl.when(pl.program_id(2) == 0)
    def _():
        s_ref[...] = jnp.zeros_like(s_ref)

    def body(c, carry):
        rows = pl.ds(pl.multiple_of(c * chunk, chunk), chunk)
        o, s_new = _gla_chunk(q_ref[rows, :] * scale, k_ref[rows, :], v_ref[rows, :],
                              la_ref[rows, :], s_ref[...], sub)
        s_ref[...] = s_new
        o_ref[rows, :] = _gla_gate_out(o, r_ref[rows, :], gn_ref[...])
        return carry

    lax.fori_loop(0, q_ref.shape[0] // chunk, body, 0)


def gla_prompt(qk, vr, la, gn, s_all, layer, *, nb, seq, heads, dk, dv, n_layers, tb=512):
    m = qk.shape[0]
    chunk = min(GLA_CHUNK, seq)
    tb = min(tb, seq)
    nt = seq // tb
    row = lambda b, h, t: b * nt + t
    in_specs = [
        pl.BlockSpec((tb, dk), lambda b, h, t: (row(b, h, t), h)),
        pl.BlockSpec((tb, dk), lambda b, h, t: (row(b, h, t), heads + h)),
        pl.BlockSpec((tb, dv), lambda b, h, t: (row(b, h, t), h)),
        pl.BlockSpec((tb, dv), lambda b, h, t: (row(b, h, t), heads + h)),
        pl.BlockSpec((tb, dk), lambda b, h, t: (row(b, h, t), h)),
        pl.BlockSpec((1, dv), lambda b, h, t: (0, 0)),
    ]
    args = [qk, qk, vr, vr, la, gn]
    aliases = {}
    if s_all is not None:
        in_specs.append(pl.BlockSpec(memory_space=pl.ANY))
        args.append(s_all)
        aliases = {len(args) - 1: 1}
    kern = functools.partial(_gla_prompt_kernel, chunk=chunk, sub=min(GLA_SUB, chunk), scale=dk ** -0.5)
    if s_all is not None:
        kern = functools.partial(_drop_arg, kern, 6)
    return pl.pallas_call(
        kern,
        grid=(nb, heads, nt),
        in_specs=in_specs,
        out_specs=[
            pl.BlockSpec((tb, dv), lambda b, h, t: (row(b, h, t), h)),
            pl.BlockSpec((None, None, None, dk, dv), lambda b, h, t: (layer, b, h, 0, 0)),
        ],
        out_shape=[
            jax.ShapeDtypeStruct((m, heads * dv), BF16),
            jax.ShapeDtypeStruct((n_layers, nb, heads, dk, dv), F32),
        ],
        input_output_aliases=aliases,
        compiler_params=_cparams("parallel", "parallel", "arbitrary"),
        name="gla_prompt",
    )(*args)


def _drop_arg(fn, idx, *refs):
    return fn(*refs[:idx], *refs[idx + 1:])


def _drop_args(fn, idxs, *refs):
    return fn(*[r for i, r in enumerate(refs) if i not in idxs])


def _gla_sample_kernel(q_ref, k_ref, v_ref, r_ref, la_ref, gn_ref, s0_ref, o_ref, s_ref,
                       *, seq, gb, heads, dk, dv, scale):
    for g in range(gb):
        rows = slice(g * seq, (g + 1) * seq)
        for h in range(heads):
            kc = slice(h * dk, (h + 1) * dk)
            vc = slice(h * dv, (h + 1) * dv)
            o, s_new = _gla_chunk(q_ref[rows, kc] * scale, k_ref[rows, kc], v_ref[rows, vc],
                                  la_ref[rows, kc], s0_ref[g, h], seq)
            s_ref[g, h] = s_new
            o_ref[rows, vc] = _gla_gate_out(o, r_ref[rows, vc], gn_ref[...])


def gla_sample(qk, vr, la, gn, state, og, s_all, layer, *, row0, nb, seq, heads, dk, dv, gb=2):
    m = qk.shape[0]
    n_layers = state.shape[0]
    tb = gb * seq
    blk0 = row0 // tb
    in_specs = [
        pl.BlockSpec((tb, heads * dk), lambda i: (blk0 + i, 0)),
        pl.BlockSpec((tb, heads * dk), lambda i: (blk0 + i, 1)),
        pl.BlockSpec((tb, heads * dv), lambda i: (blk0 + i, 0)),
        pl.BlockSpec((tb, heads * dv), lambda i: (blk0 + i, 1)),
        pl.BlockSpec((tb, heads * dk), lambda i: (blk0 + i, 0)),
        pl.BlockSpec((1, dv), lambda i: (0, 0)),
        pl.BlockSpec((None, gb, heads, dk, dv), lambda i: (layer, i, 0, 0, 0)),
        pl.BlockSpec(memory_space=pl.ANY),
    ]
    args = [qk, qk, vr, vr, la, gn, state, og]
    aliases = {7: 0}
    drop = [7]
    if s_all is not None:
        in_specs.append(pl.BlockSpec(memory_space=pl.ANY))
        args.append(s_all)
        aliases[8] = 1
        drop.append(8)
    kern = functools.partial(_gla_sample_kernel, seq=seq, gb=gb, heads=heads, dk=dk, dv=dv,
                             scale=dk ** -0.5)
    kern = functools.partial(_drop_args, kern, tuple(drop))
    return pl.pallas_call(
        kern,
        grid=(nb // gb,),
        in_specs=in_specs,
        out_specs=[
            pl.BlockSpec((tb, heads * dv), lambda i: (blk0 + i, 0)),
            pl.BlockSpec((None, gb, heads, dk, dv), lambda i: (layer, i, 0, 0, 0)),
        ],
        out_shape=[
            jax.ShapeDtypeStruct((m, heads * dv), BF16),
            jax.ShapeDtypeStruct((n_layers, nb, heads, dk, dv), F32),
        ],
        input_output_aliases=aliases,
        compiler_params=_cparams("parallel"),
        name="gla_sample",
    )(*args)


def _mla_pre_kernel(x_ref, g_ref, win_ref, gq_ref, gkv_ref, wqb_ref, wuk_ref, cos_ref, sin_ref,
                    c_ref, kr_ref, cb_ref, ct_ref, krb_ref, ql_ref, qr_ref,
                    *, heads, q_lora, kv_lora, nope, rope, scale):
    h = _rms(x_ref[...], g_ref[...]).astype(BF16)
    p = _dot(h, win_ref[...])
    cos = cos_ref[...]
    sin = sin_ref[...]
    c = _rms(p[:, q_lora:q_lora + kv_lora], gkv_ref[...])
    c_ref[...] = c
    cb_ref[...] = c.astype(BF16)
    ct_ref[...] = c.T.astype(BF16)
    o = q_lora + kv_lora
    kr = p[:, o:o + rope] * cos[:, :rope] + p[:, o + rope:o + 2 * rope] * sin[:, :rope]
    kr_ref[...] = kr
    krb_ref[...] = kr.astype(BF16)
    cqn = _rms(p[:, :q_lora], gq_ref[...]).astype(BF16)
    q = _dot(cqn, wqb_ref[...])
    n0 = heads * nope
    n1 = n0 + heads * rope
    pair = 2 * rope
    for hp in range(heads * rope // pair):
        a = q[:, n0 + hp * pair:n0 + (hp + 1) * pair]
        b = q[:, n1 + hp * pair:n1 + (hp + 1) * pair]
        two = ((a * cos + b * sin) * scale).astype(BF16)
        qr_ref[2 * hp] = two[:, :rope]
        qr_ref[2 * hp + 1] = two[:, rope:]
    for hd in range(heads):
        qn = q[:, hd * nope:(hd + 1) * nope].astype(BF16)
        ql_ref[hd] = (_dot(qn, wuk_ref[hd]) * scale).astype(BF16)


def mla_pre(x, g, win_ext, gq, gkv, wqb_ext, wuk_t, cos, sin, *, heads, q_lora, kv_lora, nope, rope,
            tm=256):
    m, d = x.shape
    tm = min(tm, m)
    full = lambda a: pl.BlockSpec(a.shape, lambda i: (0,) * a.ndim)
    rows = lambda n: pl.BlockSpec((tm, n), lambda i: (i, 0))
    hrows = lambda n: pl.BlockSpec((heads, tm, n), lambda i: (0, i, 0))
    kern = functools.partial(_mla_pre_kernel, heads=heads, q_lora=q_lora, kv_lora=kv_lora, nope=nope,
                             rope=rope, scale=(nope + rope) ** -0.5)
    return pl.pallas_call(
        kern,
        grid=(m // tm,),
        in_specs=[rows(d), full(g), full(win_ext), full(gq), full(gkv), full(wqb_ext), full(wuk_t),
                  rows(cos.shape[1]), rows(sin.shape[1])],
        out_specs=[rows(kv_lora), rows(rope), rows(kv_lora),
                   pl.BlockSpec((kv_lora, tm), lambda i: (0, i)), rows(rope),
                   hrows(kv_lora), hrows(rope)],
        out_shape=[
            jax.ShapeDtypeStruct((m, kv_lora), F32),
            jax.ShapeDtypeStruct((m, rope), F32),
            jax.ShapeDtypeStruct((m, kv_lora), BF16),
            jax.ShapeDtypeStruct((kv_lora, m), BF16),
            jax.ShapeDtypeStruct((m, rope), BF16),
            jax.ShapeDtypeStruct((heads, m, kv_lora), BF16),
            jax.ShapeDtypeStruct((heads, m, rope), BF16),
        ],
        compiler_params=_cparams("parallel"),
        name="mla_pre",
    )(x, g, win_ext, gq, gkv, wqb_ext, wuk_t, cos, sin)


def _flash_update(s, v, m_ref, l_ref, acc_ref):
    m_prev = m_ref[...]
    m_new = jnp.maximum(m_prev, jnp.max(s, axis=-1, keepdims=True))
    alpha = jnp.exp(m_prev - m_new)
    p = jnp.exp(s - m_new)
    l_ref[...] = alpha * l_ref[...] + jnp.sum(p, axis=-1, keepdims=True)
    acc_ref[...] = alpha * acc_ref[...] + _dot(p.astype(BF16), v)
    m_ref[...] = m_new


def _flash_update_t(st, vt, m_ref, l_ref, acc_ref):
    m_prev = m_ref[...]
    m_new = jnp.maximum(m_prev, jnp.max(st, axis=0, keepdims=True))
    alpha = jnp.exp(m_prev - m_new)
    p = jnp.exp(st - m_new)
    l_ref[...] = alpha * l_ref[...] + jnp.sum(p, axis=0, keepdims=True)
    acc_ref[...] = alpha * acc_ref[...] + _dot(vt, p.astype(BF16))
    m_ref[...] = m_new


def _flash_init(m_ref, l_ref, acc_ref):
    m_ref[...] = jnp.full_like(m_ref, NEG_BIG)
    l_ref[...] = jnp.zeros_like(l_ref)
    acc_ref[...] = jnp.zeros_like(acc_ref)


def _mla_attn_prompt_kernel(qi_ref, ki_ref, q_ref, qr_ref, c_ref, kr_ref, ct_ref, o_ref,
                            m_ref, l_ref, acc_ref, *, qb, tk):
    step = pl.program_id(1)
    qi = qi_ref[step]
    ki = ki_ref[step]
    heads = q_ref.shape[0]
    r = heads * qb

    @pl.when(ki == 0)
    def _():
        _flash_init(m_ref, l_ref, acc_ref)

    q = q_ref[...].reshape(r, q_ref.shape[2])
    qr = qr_ref[...].reshape(r, qr_ref.shape[2])
    st = _dot_nt(c_ref[...], q) + _dot_nt(kr_ref[...], qr)
    crosses_diagonal = ki * tk + tk - 1 > qi * qb

    @pl.when(crosses_diagonal)
    def _():
        kpos = ki * tk + lax.broadcasted_iota(jnp.int32, st.shape, 0)
        qpos = qi * qb + (lax.broadcasted_iota(jnp.int32, st.shape, 1) & (qb - 1))
        _flash_update_t(jnp.where(kpos <= qpos, st, NEG_BIG), ct_ref[...], m_ref, l_ref, acc_ref)

    @pl.when(jnp.logical_not(crosses_diagonal))
    def _():
        _flash_update_t(st, ct_ref[...], m_ref, l_ref, acc_ref)

    @pl.when(ki == (qi * qb + qb - 1) // tk)
    def _():
        o = (acc_ref[...] * (1.0 / l_ref[...])).T
        for hd in range(heads):
            o_ref[hd] = o[hd * qb:(hd + 1) * qb, :].astype(o_ref.dtype)


def mla_attn_prompt(ql, qr, cb, krb, ct, *, nb, seq, qb=32, tk=512):
    heads, m, kv_lora = ql.shape
    rope = qr.shape[2]
    qb, tk = min(qb, seq), min(tk, seq)
    assert qb & (qb - 1) == 0
    r = qb * heads
    nq, nk = seq // qb, seq // tk
    pairs = [(qi, ki) for qi in range(nq) for ki in range((qi * qb + qb - 1) // tk + 1)]
    qi_tab = jnp.asarray([p[0] for p in pairs], jnp.int32)
    ki_tab = jnp.asarray([p[1] for p in pairs], jnp.int32)
    qmap = lambda b, s, qt, kt: (0, b * nq + qt[s], 0)
    kern = functools.partial(_mla_attn_prompt_kernel, qb=qb, tk=tk)
    grid_spec = pltpu.PrefetchScalarGridSpec(
        num_scalar_prefetch=2,
        grid=(nb, len(pairs)),
        in_specs=[
            pl.BlockSpec((heads, qb, kv_lora), qmap),
            pl.BlockSpec((heads, qb, rope), qmap),
            pl.BlockSpec((tk, kv_lora), lambda b, s, qt, kt: (b * nk + kt[s], 0)),
            pl.BlockSpec((tk, rope), lambda b, s, qt, kt: (b * nk + kt[s], 0)),
            pl.BlockSpec((kv_lora, tk), lambda b, s, qt, kt: (0, b * nk + kt[s])),
        ],
        out_specs=pl.BlockSpec((heads, qb, kv_lora), qmap),
        scratch_shapes=[pltpu.VMEM((1, r), F32), pltpu.VMEM((1, r), F32), pltpu.VMEM((kv_lora, r), F32)],
    )
    return pl.pallas_call(
        kern,
        grid_spec=grid_spec,
        out_shape=jax.ShapeDtypeStruct((heads, m, kv_lora), BF16),
        compiler_params=_cparams("parallel", "arbitrary"),
        name="mla_attn_prompt",
    )(qi_tab, ki_tab, ql, qr, cb, krb, ct)


def _mla_attn_sample_kernel(pt_ref, q_ref, qr_ref, cn_ref, krn_ref, *rest, pp, page, seq):
    del pt_ref
    lat_refs = rest[:pp]
    krt_refs = rest[pp:2 * pp]
    o_ref = rest[2 * pp + 1]
    m_ref, l_ref, acc_ref, kbuf_ref, krtbuf_ref, q2_ref, qr2_ref = rest[2 * pp + 2:]
    g = pl.program_id(1)
    heads = q_ref.shape[0]

    @pl.when(g == 0)
    def _():
        _flash_init(m_ref, l_ref, acc_ref)
        q2_ref[...] = jnp.concatenate(
            [q_ref[hd].astype(F32) for hd in range(heads)], axis=0).astype(BF16)
        qr2_ref[...] = jnp.concatenate(
            [qr_ref[hd].astype(F32) for hd in range(heads)], axis=0).astype(BF16)

    for i in range(pp):
        kbuf_ref[i * page:(i + 1) * page, :] = lat_refs[i][...].astype(BF16)
        krtbuf_ref[:, i * page:(i + 1) * page] = krt_refs[i][...].astype(BF16)
    q = q2_ref[...]
    qr = qr2_ref[...]
    half = pp * page // 2
    for ch in range(2):
        kb = kbuf_ref[ch * half:(ch + 1) * half, :]
        s = _dot_nt(q, kb) + _dot(qr, krtbuf_ref[:, ch * half:(ch + 1) * half])
        _flash_update(s, kb, m_ref.at[ch], l_ref.at[ch], acc_ref.at[ch])

    @pl.when(g == pl.num_programs(1) - 1)
    def _():
        cn = cn_ref[...].astype(BF16)
        s = _dot_nt(q, cn) + _dot_nt(qr, krn_ref[...].astype(BF16))
        qtok = lax.broadcasted_iota(jnp.int32, s.shape, 0) & (seq - 1)
        ktok = lax.broadcasted_iota(jnp.int32, s.shape, 1)
        _flash_update(jnp.where(ktok <= qtok, s, NEG_BIG), cn, m_ref.at[0], l_ref.at[0], acc_ref.at[0])
        m0, m1 = m_ref[0], m_ref[1]
        mx = jnp.maximum(m0, m1)
        e0, e1 = jnp.exp(m0 - mx), jnp.exp(m1 - mx)
        o = (acc_ref[0] * e0 + acc_ref[1] * e1) / (l_ref[0] * e0 + l_ref[1] * e1)
        for hd in range(heads):
            o_ref[hd] = o[hd * seq:(hd + 1) * seq, :].astype(o_ref.dtype)


def mla_attn_sample(page_table, ql, qr, c_new, kr_new, cache_lat, cache_krt, o_lat, layer,
                    *, row0, nb, seq, pp=16):
    heads, _, kv_lora = ql.shape
    rope = qr.shape[2]
    n_pages = page_table.shape[1]
    page = cache_lat.shape[2]
    pp = min(pp, n_pages)
    assert seq & (seq - 1) == 0 and pp % 2 == 0
    r = seq * heads
    blk0 = row0 // seq
    kern = functools.partial(_mla_attn_sample_kernel, pp=pp, page=page, seq=seq)
    pmap = lambda i, b, g, pt: (layer, pt[b, g * pp + i], 0, 0)
    lat_specs = [pl.BlockSpec((None, None, page, kv_lora), functools.partial(pmap, i)) for i in range(pp)]
    krt_specs = [pl.BlockSpec((None, None, rope, page), functools.partial(pmap, i)) for i in range(pp)]
    grid_spec = pltpu.PrefetchScalarGridSpec(
        num_scalar_prefetch=1,
        grid=(nb, n_pages // pp),
        in_specs=[
            pl.BlockSpec((heads, seq, kv_lora), lambda b, g, pt: (0, blk0 + b, 0)),
            pl.BlockSpec((heads, seq, rope), lambda b, g, pt: (0, blk0 + b, 0)),
            pl.BlockSpec((seq, kv_lora), lambda b, g, pt: (blk0 + b, 0)),
            pl.BlockSpec((seq, rope), lambda b, g, pt: (blk0 + b, 0)),
            *lat_specs, *krt_specs,
            pl.BlockSpec(memory_space=pl.ANY),
        ],
        out_specs=pl.BlockSpec((heads, seq, kv_lora), lambda b, g, pt: (0, blk0 + b, 0)),
        scratch_shapes=[
            pltpu.VMEM((2, r, 1), F32), pltpu.VMEM((2, r, 1), F32), pltpu.VMEM((2, r, kv_lora), F32),
            pltpu.VMEM((pp * page, kv_lora), BF16), pltpu.VMEM((rope, pp * page), BF16),
            pltpu.VMEM((r, kv_lora), BF16), pltpu.VMEM((r, rope), BF16),
        ],
    )
    return pl.pallas_call(
        kern,
        grid_spec=grid_spec,
        out_shape=jax.ShapeDtypeStruct(o_lat.shape, o_lat.dtype),
        input_output_aliases={5 + 2 * pp: 0},
        compiler_params=_cparams("parallel", "arbitrary"),
        name="mla_attn_sample",
    )(page_table, ql, qr, c_new, kr_new, *([cache_lat] * pp), *([cache_krt] * pp), o_lat)


def _mla_post_kernel(ol_ref, wuv_ref, wo_ref, x_ref, o_ref, ob_ref, *, vdim):
    @pl.when(pl.program_id(1) == 0)
    def _():
        for hd in range(ol_ref.shape[0]):
            ob_ref[:, hd * vdim:(hd + 1) * vdim] = _dot(ol_ref[hd], wuv_ref[hd]).astype(BF16)

    o_ref[...] = x_ref[...] + _dot(ob_ref[...], wo_ref[...])


def mla_post(o_lat, wuv, wo, x, *, tm=256, tn=1024):
    m, d = x.shape
    heads, _, kv_lora = o_lat.shape
    vdim = wuv.shape[2]
    tm, tn = min(tm, m), min(tn, d)
    kern = functools.partial(_mla_post_kernel, vdim=vdim)
    return pl.pallas_call(
        kern,
        grid=(m // tm, d // tn),
        in_specs=[
            pl.BlockSpec((heads, tm, kv_lora), lambda i, j: (0, i, 0)),
            pl.BlockSpec(wuv.shape, lambda i, j: (0, 0, 0)),
            pl.BlockSpec((heads * vdim, tn), lambda i, j: (0, j)),
            pl.BlockSpec((tm, tn), lambda i, j: (i, j)),
        ],
        out_specs=pl.BlockSpec((tm, tn), lambda i, j: (i, j)),
        out_shape=jax.ShapeDtypeStruct((m, d), F32),
        scratch_shapes=[pltpu.VMEM((tm, heads * vdim), BF16)],
        compiler_params=_cparams("parallel", "arbitrary"),
        name="mla_post",
    )(o_lat, wuv, wo, x)


def _rot_half_cols(w, half):
    return jnp.concatenate([-w[..., half:], w[..., :half]], axis=-1)


def gla_layer(x, ln_g, w_in, w_gate_up, b_gate, g_norm, w_out, state_gla, states, layer, dims):
    nbp, lp, nbs, ls, heads, dk, dv, rank = dims
    nq, nv = heads * dk, heads * dv
    mp = nbp * lp
    g = ln_g.reshape(1, -1)
    w_qk = w_in[:, :2 * nq].astype(BF16)
    w_vr = w_in[:, 2 * nq:2 * nq + 2 * nv].astype(BF16)
    lane = 128
    wa = jnp.pad(w_in[:, 2 * nq + 2 * nv:], ((0, 0), (0, lane - rank))).astype(BF16)
    wg = jnp.pad(w_gate_up, ((0, lane - rank), (0, 0))).astype(BF16)
    qk = norm_matmul(x, g, w_qk, F32)
    vr = norm_matmul(x, g, w_vr, BF16)
    la = gla_gate(x, g, wa, wg, b_gate.reshape(1, -1))
    gn = g_norm.reshape(1, -1)
    sp_all, ss_all = states
    og, sp_all = gla_prompt(qk, vr, la, gn, sp_all, layer, nb=nbp, seq=lp, heads=heads, dk=dk, dv=dv,
                            n_layers=state_gla.shape[0])
    og, ss_all = gla_sample(qk, vr, la, gn, state_gla, og, ss_all, layer, row0=mp, nb=nbs, seq=ls,
                            heads=heads, dk=dk, dv=dv)
    x = matmul_res(og, w_out.astype(BF16), x)
    return x, (sp_all, ss_all)


def mla_layer(x, ln_g, w_in, g_q, w_qb, g_kv, w_uk, w_uv, w_out, cache_lat, cache_krt, page_table,
              cos, sin, layer, dims):
    nbp, lp, nbs, ls, heads, q_lora, kv_lora, nope, rope = dims
    mp = nbp * lp
    half = rope // 2
    o = q_lora + kv_lora
    win_ext = jnp.concatenate([w_in, _rot_half_cols(w_in[:, o:o + rope], half)], axis=1).astype(BF16)
    wqb = w_qb.reshape(q_lora, heads, nope + rope)
    w_rope = wqb[:, :, nope:]
    wqb_ext = jnp.concatenate([
        wqb[:, :, :nope].reshape(q_lora, heads * nope),
        w_rope.reshape(q_lora, heads * rope),
        _rot_half_cols(w_rope, half).reshape(q_lora, heads * rope)], axis=1).astype(BF16)
    wuk_t = jnp.transpose(w_uk, (1, 2, 0)).astype(BF16)
    wuv = jnp.transpose(w_uv, (1, 0, 2)).astype(BF16)
    c_new, kr_new, cb, ct, krb, ql, qr = mla_pre(
        x, ln_g.reshape(1, -1), win_ext, g_q.reshape(1, -1), g_kv.reshape(1, -1), wqb_ext, wuk_t,
        cos, sin, heads=heads, q_lora=q_lora, kv_lora=kv_lora, nope=nope, rope=rope)
    o_lat = mla_attn_prompt(ql, qr, cb, krb, ct, nb=nbp, seq=lp)
    o_lat = mla_attn_sample(page_table, ql, qr, c_new, kr_new, cache_lat, cache_krt, o_lat, layer,
                            row0=mp, nb=nbs, seq=ls)
    x = mla_post(o_lat, wuv, w_out.astype(BF16), x)
    return x, c_new, kr_new


def kernel(x_prompt, x_sample, state_gla, cache_mla_latent, cache_mla_krope, page_table, ln_mixer_g, ln_ffn_g, gla_w_in, gla_w_gate_up, gla_b_gate, gla_norm_g, gla_w_out, mla_w_in, mla_q_norm_g, mla_w_qb, mla_kv_norm_g, mla_w_uk, mla_w_uv, mla_w_out, ffn_w_gate, ffn_w_up, ffn_w_down, final_norm_g):
    nbp, lp, d = x_prompt.shape
    nbs, ls, _ = x_sample.shape
    mp, ms = nbp * lp, nbs * ls
    depth = ln_mixer_g.shape[0]
    heads_g, dk, dv = state_gla.shape[2:]
    rank = gla_w_gate_up.shape[1]
    kv_lora, heads_m, nope = mla_w_uk.shape[1:]
    q_lora = mla_q_norm_g.shape[1]
    rope = cache_mla_krope.shape[3]
    past_len = page_table.shape[1] * cache_mla_latent.shape[2]

    x = jnp.concatenate([x_prompt.reshape(mp, d), x_sample.reshape(ms, d)], axis=0)

    half = rope // 2
    inv = 1.0 / (ROPE_THETA ** (jnp.arange(half, dtype=F32) / half))
    pos = jnp.concatenate([jnp.tile(jnp.arange(lp, dtype=jnp.int32), nbp),
                           jnp.tile(past_len + jnp.arange(ls, dtype=jnp.int32), nbs)])
    ang = pos.astype(F32)[:, None] * inv[None, :]
    cos = jnp.tile(jnp.cos(ang), (1, 2 * rope // half))
    sin = jnp.tile(jnp.sin(ang), (1, 2 * rope // half))

    gla_dims = (nbp, lp, nbs, ls, heads_g, dk, dv, rank)
    mla_dims = (nbp, lp, nbs, ls, heads_m, q_lora, kv_lora, nope, rope)
    cache_krt = jnp.swapaxes(cache_mla_krope, 2, 3)
    gla_states = (None, None)
    mla_c, mla_kr = [], []
    for i in range(depth):
        j = i // 2
        if i % 2 == 0:
            x, gla_states = gla_layer(x, ln_mixer_g[i], gla_w_in[j], gla_w_gate_up[j], gla_b_gate[j],
                                      gla_norm_g[j], gla_w_out[j], state_gla, gla_states, j, gla_dims)
        else:
            x, c_new, kr_new = mla_layer(x, ln_mixer_g[i], mla_w_in[j], mla_q_norm_g[j], mla_w_qb[j],
                                         mla_kv_norm_g[j], mla_w_uk[j], mla_w_uv[j], mla_w_out[j],
                                         cache_mla_latent, cache_krt, page_table, cos, sin, j,
                                         mla_dims)
            mla_c.append(c_new)
            mla_kr.append(kr_new)
        x = ffn(x, ln_ffn_g[i].reshape(1, -1), ffn_w_gate[i].astype(BF16), ffn_w_up[i].astype(BF16),
                ffn_w_down[i].astype(BF16))

    fg = final_norm_g.reshape(1, -1)
    y_prompt = rmsnorm_rows(x, fg, 0, mp).reshape(nbp, lp, d)
    y_sample = rmsnorm_rows(x, fg, mp, ms).reshape(nbs, ls, d)
    c_all = jnp.stack(mla_c)
    kr_all = jnp.stack(mla_kr)
    return (y_prompt, y_sample, gla_states[0], gla_states[1],
            c_all[:, :mp].reshape(-1, nbp, lp, kv_lora), kr_all[:, :mp].reshape(-1, nbp, lp, rope),
            c_all[:, mp:].reshape(-1, nbs, ls, kv_lora), kr_all[:, mp:].reshape(-1, nbs, ls, rope))
```

```python
import functools

import jax
import jax.numpy as jnp
from jax import lax
from jax.experimental import pallas as pl
from jax.experimental.pallas import tpu as pltpu

F32 = jnp.float32
BF16 = jnp.bfloat16

RMS_EPS = 1e-6
ROPE_THETA = 10000.0
GLA_GATE_TEMP = 16.0
GLA_CHUNK = 64
GLA_SUB = 16
EXP_CLAMP = 80.0
NEG_BIG = -1e30

V7X_VMEM_BYTES = 64 * 1024 * 1024
VMEM_LIMIT_BYTES = V7X_VMEM_BYTES - 8 * 1024 * 1024


def _cparams(*sem):
    return pltpu.CompilerParams(dimension_semantics=sem, vmem_limit_bytes=VMEM_LIMIT_BYTES)


def _rms(x, g):
    ms = jnp.mean(x * x, axis=-1, keepdims=True)
    return x * lax.rsqrt(ms + RMS_EPS) * g


def _silu(x):
    return x * jax.nn.sigmoid(x)


def _dot(a, b):
    return jnp.dot(a, b, preferred_element_type=F32)


def _dot_nt(a, b):
    return lax.dot_general(a, b, (((1,), (1,)), ((), ())), preferred_element_type=F32)


def _dot_tn(a, b):
    return lax.dot_general(a, b, (((0,), (0,)), ((), ())), preferred_element_type=F32)


def _norm_matmul_kernel(x_ref, g_ref, w_ref, o_ref, h_ref):
    @pl.when(pl.program_id(1) == 0)
    def _():
        h_ref[...] = _rms(x_ref[...], g_ref[...]).astype(BF16)

    o_ref[...] = _dot(h_ref[...], w_ref[...]).astype(o_ref.dtype)


def norm_matmul(x, g, w, out_dtype, *, tm=512, tn=1024):
    m, d = x.shape
    n = w.shape[1]
    tm, tn = min(tm, m), min(tn, n)
    return pl.pallas_call(
        _norm_matmul_kernel,
        grid=(m // tm, n // tn),
        in_specs=[
            pl.BlockSpec((tm, d), lambda i, j: (i, 0)),
            pl.BlockSpec((1, d), lambda i, j: (0, 0)),
            pl.BlockSpec((d, tn), lambda i, j: (0, j)),
        ],
        out_specs=pl.BlockSpec((tm, tn), lambda i, j: (i, j)),
        out_shape=jax.ShapeDtypeStruct((m, n), out_dtype),
        scratch_shapes=[pltpu.VMEM((tm, d), BF16)],
        compiler_params=_cparams("parallel", "arbitrary"),
        name="norm_matmul",
    )(x, g, w)


def _matmul_res_kernel(a_ref, w_ref, r_ref, o_ref):
    o_ref[...] = r_ref[...] + _dot(a_ref[...], w_ref[...])


def matmul_res(a, w, res, *, tm=512, tn=1024):
    m, k = a.shape
    n = w.shape[1]
    tm, tn = min(tm, m), min(tn, n)
    return pl.pallas_call(
        _matmul_res_kernel,
        grid=(m // tm, n // tn),
        in_specs=[
            pl.BlockSpec((tm, k), lambda i, j: (i, 0)),
            pl.BlockSpec((k, tn), lambda i, j: (0, j)),
            pl.BlockSpec((tm, tn), lambda i, j: (i, j)),
        ],
        out_specs=pl.BlockSpec((tm, tn), lambda i, j: (i, j)),
        out_shape=jax.ShapeDtypeStruct((m, n), F32),
        compiler_params=_cparams("parallel", "arbitrary"),
        name="matmul_res",
    )(a, w, res)


def _ffn_kernel(x_ref, g_ref, wg_ref, wu_ref, wd_ref, o_ref, h_ref):
    @pl.when(pl.program_id(1) == 0)
    def _():
        x = x_ref[...]
        h_ref[...] = _rms(x, g_ref[...]).astype(BF16)
        o_ref[...] = x

    h = h_ref[...]
    a = (_silu(_dot(h, wg_ref[...])) * _dot(h, wu_ref[...])).astype(BF16)
    o_ref[...] += _dot(a, wd_ref[...])


def ffn(x, g, wg, wu, wd, *, tm=512, tf=512):
    m, d = x.shape
    f = wg.shape[1]
    tm, tf = min(tm, m), min(tf, f)
    return pl.pallas_call(
        _ffn_kernel,
        grid=(m // tm, f // tf),
        in_specs=[
            pl.BlockSpec((tm, d), lambda i, j: (i, 0)),
            pl.BlockSpec((1, d), lambda i, j: (0, 0)),
            pl.BlockSpec((d, tf), lambda i, j: (0, j)),
            pl.BlockSpec((d, tf), lambda i, j: (0, j)),
            pl.BlockSpec((tf, d), lambda i, j: (j, 0)),
        ],
        out_specs=pl.BlockSpec((tm, d), lambda i, j: (i, 0)),
        out_shape=jax.ShapeDtypeStruct((m, d), F32),
        scratch_shapes=[pltpu.VMEM((tm, d), BF16)],
        compiler_params=_cparams("parallel", "arbitrary"),
        name="ffn",
    )(x, g, wg, wu, wd)


def _rmsnorm_kernel(x_ref, g_ref, o_ref):
    o_ref[...] = _rms(x_ref[...], g_ref[...])


def rmsnorm_rows(x, g, row0, nrows, *, tm=512):
    d = x.shape[1]
    tm = min(tm, nrows)
    blk0 = row0 // tm
    return pl.pallas_call(
        _rmsnorm_kernel,
        grid=(nrows // tm,),
        in_specs=[
            pl.BlockSpec((tm, d), lambda i: (blk0 + i, 0)),
            pl.BlockSpec((1, d), lambda i: (0, 0)),
        ],
        out_specs=pl.BlockSpec((tm, d), lambda i: (i, 0)),
        out_shape=jax.ShapeDtypeStruct((nrows, d), F32),
        compiler_params=_cparams("parallel"),
        name="final_rmsnorm",
    )(x, g)


def _gla_gate_kernel(x_ref, g_ref, wa_ref, wg_ref, bg_ref, o_ref):
    h = _rms(x_ref[...], g_ref[...]).astype(BF16)
    a = _dot(h, wa_ref[...]).astype(BF16)
    z = _dot(a, wg_ref[...]) + bg_ref[...]
    log_sig = jnp.minimum(z, 0.0) - jnp.log1p(jnp.exp(-jnp.abs(z)))
    o_ref[...] = log_sig * (1.0 / GLA_GATE_TEMP)


def gla_gate(x, g, wa, wg, bg, *, tm=512):
    m, d = x.shape
    n = wg.shape[1]
    tm = min(tm, m)
    return pl.pallas_call(
        _gla_gate_kernel,
        grid=(m // tm,),
        in_specs=[
            pl.BlockSpec((tm, d), lambda i: (i, 0)),
            pl.BlockSpec((1, d), lambda i: (0, 0)),
            pl.BlockSpec(wa.shape, lambda i: (0, 0)),
            pl.BlockSpec(wg.shape, lambda i: (0, 0)),
            pl.BlockSpec((1, n), lambda i: (0, 0)),
        ],
        out_specs=pl.BlockSpec((tm, n), lambda i: (i, 0)),
        out_shape=jax.ShapeDtypeStruct((m, n), F32),
        compiler_params=_cparams("parallel"),
        name="gla_gate",
    )(x, g, wa, wg, bg)


def _cumsum_rows(la):
    c = la.shape[0]
    if c <= 8:
        rows = lax.broadcasted_iota(jnp.int32, la.shape, 0)
        out = jnp.zeros_like(la)
        for j in range(c):
            out = out + jnp.where(rows >= j, la[j:j + 1, :], 0.0)
        return out
    r = lax.broadcasted_iota(jnp.int32, (c, c), 0)
    s = lax.broadcasted_iota(jnp.int32, (c, c), 1)
    tri = jnp.where(s <= r, 1.0, 0.0).astype(BF16)
    hi = la.astype(BF16)
    lo = (la - hi.astype(F32)).astype(BF16)
    return _dot(tri, hi) + _dot(tri, lo)


def _gla_chunk(qs, k, v, la, s, sub):
    c, dk = qs.shape
    b = _cumsum_rows(la)
    b_last = b[c - 1:c, :]
    o = _dot((qs * jnp.exp(b)).astype(BF16), s.astype(BF16))
    col = lax.broadcasted_iota(jnp.int32, (sub, c), 1)
    row = lax.broadcasted_iota(jnp.int32, (sub, c), 0)
    parts = []
    for i in range(c // sub):
        lo = i * sub
        ref = b[lo - 1:lo, :] if i else jnp.zeros((1, dk), F32)
        qi = (qs[lo:lo + sub, :] * jnp.exp(b[lo:lo + sub, :] - ref)).astype(BF16)
        kd = (k * jnp.exp(jnp.minimum(ref - b, EXP_CLAMP))).astype(BF16)
        parts.append(jnp.where(col <= row + lo, _dot_nt(qi, kd), 0.0))
    att = parts[0] if len(parts) == 1 else jnp.concatenate(parts, axis=0)
    o = o + _dot(att.astype(BF16), v)
    kdl = (k * jnp.exp(b_last - b)).astype(BF16)
    decay_col = jnp.exp(b[c - 8:c, :].T[:, 7:8])
    s_new = decay_col * s + _dot_tn(kdl, v)
    return o, s_new


def _gla_gate_out(o, r, gn):
    return (_rms(o, gn) * _silu(r.astype(F32))).astype(BF16)


def _gla_prompt_kernel(q_ref, k_ref, v_ref, r_ref, la_ref, gn_ref, o_ref, s_ref,
                       *, chunk, sub, scale, unroll):
    @pl.when(pl.program_id(2) == 0)
    def _():
        s_ref[...] = jnp.zeros_like(s_ref)

    hg, dk, dv = s_ref.shape

    def body(c, carry):
        rows = pl.ds(pl.multiple_of(c * chunk, chunk), chunk)
        for j in range(hg):
            kc = slice(j * dk, (j + 1) * dk)
            vc = slice(j * dv, (j + 1) * dv)
            o, s_new = _gla_chunk(q_ref[rows, kc] * scale, k_ref[rows, kc], v_ref[rows, vc],
                                  la_ref[rows, kc], s_ref[j], sub)
            s_ref[j] = s_new
            o_ref[rows, vc] = _gla_gate_out(o, r_ref[rows, vc], gn_ref[...])
        return carry

    lax.fori_loop(0, q_ref.shape[0] // chunk, body, 0, unroll=unroll)


def gla_prompt(qk, vr, la, gn, s_all, layer, *, nb, seq, heads, dk, dv, n_layers, tb=512, hg=4,
               unroll=1):
    m = qk.shape[0]
    chunk = min(GLA_CHUNK, seq)
    tb = min(tb, seq)
    nt = seq // tb
    ngrp = heads // hg
    assert ngrp * hg == heads
    row = lambda b, h, t: b * nt + t
    in_specs = [
        pl.BlockSpec((tb, hg * dk), lambda b, h, t: (row(b, h, t), h)),
        pl.BlockSpec((tb, hg * dk), lambda b, h, t: (row(b, h, t), ngrp + h)),
        pl.BlockSpec((tb, hg * dv), lambda b, h, t: (row(b, h, t), h)),
        pl.BlockSpec((tb, hg * dv), lambda b, h, t: (row(b, h, t), ngrp + h)),
        pl.BlockSpec((tb, hg * dk), lambda b, h, t: (row(b, h, t), h)),
        pl.BlockSpec((1, dv), lambda b, h, t: (0, 0)),
    ]
    args = [qk, qk, vr, vr, la, gn]
    aliases = {}
    if s_all is not None:
        in_specs.append(pl.BlockSpec(memory_space=pl.ANY))
        args.append(s_all)
        aliases = {len(args) - 1: 1}
    kern = functools.partial(_gla_prompt_kernel, chunk=chunk, sub=min(GLA_SUB, chunk), scale=dk ** -0.5,
                             unroll=unroll)
    if s_all is not None:
        kern = functools.partial(_drop_arg, kern, 6)
    return pl.pallas_call(
        kern,
        grid=(nb, ngrp, nt),
        in_specs=in_specs,
        out_specs=[
            pl.BlockSpec((tb, hg * dv), lambda b, h, t: (row(b, h, t), h)),
            pl.BlockSpec((None, None, hg, dk, dv), lambda b, h, t: (layer, b, h, 0, 0)),
        ],
        out_shape=[
            jax.ShapeDtypeStruct((m, heads * dv), BF16),
            jax.ShapeDtypeStruct((n_layers, nb, heads, dk, dv), F32),
        ],
        input_output_aliases=aliases,
        compiler_params=_cparams("parallel", "parallel", "arbitrary"),
        name="gla_prompt",
    )(*args)


def _drop_arg(fn, idx, *refs):
    return fn(*refs[:idx], *refs[idx + 1:])


def _drop_args(fn, idxs, *refs):
    return fn(*[r for i, r in enumerate(refs) if i not in idxs])


def _gla_sample_kernel(q_ref, k_ref, v_ref, r_ref, la_ref, gn_ref, s0_ref, o_ref, s_ref,
                       *, seq, gb, heads, dk, dv, scale):
    for g in range(gb):
        rows = slice(g * seq, (g + 1) * seq)
        for h in range(heads):
            kc = slice(h * dk, (h + 1) * dk)
            vc = slice(h * dv, (h + 1) * dv)
            o, s_new = _gla_chunk(q_ref[rows, kc] * scale, k_ref[rows, kc], v_ref[rows, vc],
                                  la_ref[rows, kc], s0_ref[g, h], seq)
            s_ref[g, h] = s_new
            o_ref[rows, vc] = _gla_gate_out(o, r_ref[rows, vc], gn_ref[...])


def gla_sample(qk, vr, la, gn, state, og, s_all, layer, *, row0, nb, seq, heads, dk, dv, gb=2):
    m = qk.shape[0]
    n_layers = state.shape[0]
    tb = gb * seq
    blk0 = row0 // tb
    in_specs = [
        pl.BlockSpec((tb, heads * dk), lambda i: (blk0 + i, 0)),
        pl.BlockSpec((tb, heads * dk), lambda i: (blk0 + i, 1)),
        pl.BlockSpec((tb, heads * dv), lambda i: (blk0 + i, 0)),
        pl.BlockSpec((tb, heads * dv), lambda i: (blk0 + i, 1)),
        pl.BlockSpec((tb, heads * dk), lambda i: (blk0 + i, 0)),
        pl.BlockSpec((1, dv), lambda i: (0, 0)),
        pl.BlockSpec((None, gb, heads, dk, dv), lambda i: (layer, i, 0, 0, 0)),
        pl.BlockSpec(memory_space=pl.ANY),
    ]
    args = [qk, qk, vr, vr, la, gn, state, og]
    aliases = {7: 0}
    drop = [7]
    if s_all is not None:
        in_specs.append(pl.BlockSpec(memory_space=pl.ANY))
        args.append(s_all)
        aliases[8] = 1
        drop.append(8)
    kern = functools.partial(_gla_sample_kernel, seq=seq, gb=gb, heads=heads, dk=dk, dv=dv,
                             scale=dk ** -0.5)
    kern = functools.partial(_drop_args, kern, tuple(drop))
    return pl.pallas_call(
        kern,
        grid=(nb // gb,),
        in_specs=in_specs,
        out_specs=[
            pl.BlockSpec((tb, heads * dv), lambda i: (blk0 + i, 0)),
            pl.BlockSpec((None, gb, heads, dk, dv), lambda i: (layer, i, 0, 0, 0)),
        ],
        out_shape=[
            jax.ShapeDtypeStruct((m, heads * dv), BF16),
            jax.ShapeDtypeStruct((n_layers, nb, heads, dk, dv), F32),
        ],
        input_output_aliases=aliases,
        compiler_params=_cparams("parallel"),
        name="gla_sample",
    )(*args)


def _mla_pre_kernel(x_ref, g_ref, win_ref, gq_ref, gkv_ref, wqb_ref, wuk_ref, cos_ref, sin_ref,
                    c_ref, kr_ref, cb_ref, ct_ref, krb_ref, ql_ref, qr_ref,
                    *, heads, q_lora, kv_lora, nope, rope, scale):
    h = _rms(x_ref[...], g_ref[...]).astype(BF16)
    p = _dot(h, win_ref[...])
    cos = cos_ref[...]
    sin = sin_ref[...]
    c = _rms(p[:, q_lora:q_lora + kv_lora], gkv_ref[...])
    c_ref[...] = c
    cb_ref[...] = c.astype(BF16)
    ct_ref[...] = c.T.astype(BF16)
    o = q_lora + kv_lora
    kr = p[:, o:o + rope] * cos[:, :rope] + p[:, o + rope:o + 2 * rope] * sin[:, :rope]
    kr_ref[...] = kr
    krb_ref[...] = kr.astype(BF16)
    cqn = _rms(p[:, :q_lora], gq_ref[...]).astype(BF16)
    q = _dot(cqn, wqb_ref[...])
    n0 = heads * nope
    n1 = n0 + heads * rope
    pair = 2 * rope
    for hp in range(heads * rope // pair):
        a = q[:, n0 + hp * pair:n0 + (hp + 1) * pair]
        b = q[:, n1 + hp * pair:n1 + (hp + 1) * pair]
        two = ((a * cos + b * sin) * scale).astype(BF16)
        qr_ref[2 * hp] = two[:, :rope]
        qr_ref[2 * hp + 1] = two[:, rope:]
    for hd in range(heads):
        qn = q[:, hd * nope:(hd + 1) * nope].astype(BF16)
        ql_ref[hd] = (_dot(qn, wuk_ref[hd]) * scale).astype(BF16)


def mla_pre(x, g, win_ext, gq, gkv, wqb_ext, wuk_t, cos, sin, *, heads, q_lora, kv_lora, nope, rope,
            tm=256):
    m, d = x.shape
    tm = min(tm, m)
    full = lambda a: pl.BlockSpec(a.shape, lambda i: (0,) * a.ndim)
    rows = lambda n: pl.BlockSpec((tm, n), lambda i: (i, 0))
    hrows = lambda n: pl.BlockSpec((heads, tm, n), lambda i: (0, i, 0))
    kern = functools.partial(_mla_pre_kernel, heads=heads, q_lora=q_lora, kv_lora=kv_lora, nope=nope,
                             rope=rope, scale=(nope + rope) ** -0.5)
    return pl.pallas_call(
        kern,
        grid=(m // tm,),
        in_specs=[rows(d), full(g), full(win_ext), full(gq), full(gkv), full(wqb_ext), full(wuk_t),
                  rows(cos.shape[1]), rows(sin.shape[1])],
        out_specs=[rows(kv_lora), rows(rope), rows(kv_lora),
                   pl.BlockSpec((kv_lora, tm), lambda i: (0, i)), rows(rope),
                   hrows(kv_lora), hrows(rope)],
        out_shape=[
            jax.ShapeDtypeStruct((m, kv_lora), F32),
            jax.ShapeDtypeStruct((m, rope), F32),
            jax.ShapeDtypeStruct((m, kv_lora), BF16),
            jax.ShapeDtypeStruct((kv_lora, m), BF16),
            jax.ShapeDtypeStruct((m, rope), BF16),
            jax.ShapeDtypeStruct((heads, m, kv_lora), BF16),
            jax.ShapeDtypeStruct((heads, m, rope), BF16),
        ],
        compiler_params=_cparams("parallel"),
        name="mla_pre",
    )(x, g, win_ext, gq, gkv, wqb_ext, wuk_t, cos, sin)


def _flash_update(s, v, m_ref, l_ref, acc_ref):
    m_prev = m_ref[...]
    m_new = jnp.maximum(m_prev, jnp.max(s, axis=-1, keepdims=True))
    alpha = jnp.exp(m_prev - m_new)
    p = jnp.exp(s - m_new)
    l_ref[...] = alpha * l_ref[...] + jnp.sum(p, axis=-1, keepdims=True)
    acc_ref[...] = alpha * acc_ref[...] + _dot(p.astype(BF16), v)
    m_ref[...] = m_new


def _flash_update_t(st, vt, m_ref, l_ref, acc_ref):
    m_prev = m_ref[...]
    m_new = jnp.maximum(m_prev, jnp.max(st, axis=0, keepdims=True))
    alpha = jnp.exp(m_prev - m_new)
    p = jnp.exp(st - m_new)
    l_ref[...] = alpha * l_ref[...] + jnp.sum(p, axis=0, keepdims=True)
    acc_ref[...] = alpha * acc_ref[...] + _dot(vt, p.astype(BF16))
    m_ref[...] = m_new


def _flash_init(m_ref, l_ref, acc_ref):
    m_ref[...] = jnp.full_like(m_ref, NEG_BIG)
    l_ref[...] = jnp.zeros_like(l_ref)
    acc_ref[...] = jnp.zeros_like(acc_ref)


def _mla_attn_prompt_kernel(qi_ref, ki_ref, q_ref, qr_ref, c_ref, kr_ref, ct_ref, o_ref, *stats,
                            qb, tk, nsplit):
    step = pl.program_id(1)
    qi = qi_ref[step]
    ki = ki_ref[step]
    heads = q_ref.shape[0]
    gb = qb // nsplit
    r = heads * gb
    groups = [stats[3 * g:3 * g + 3] for g in range(nsplit)]

    @pl.when(ki == 0)
    def _():
        for m_ref, l_ref, acc_ref in groups:
            _flash_init(m_ref, l_ref, acc_ref)

    def update(masked):
        for g, (m_ref, l_ref, acc_ref) in enumerate(groups):
            q = q_ref[:, g * gb:(g + 1) * gb, :].reshape(r, q_ref.shape[2])
            qr = qr_ref[:, g * gb:(g + 1) * gb, :].reshape(r, qr_ref.shape[2])
            st = _dot_nt(c_ref[...], q) + _dot_nt(kr_ref[...], qr)
            if masked:
                kpos = ki * tk + lax.broadcasted_iota(jnp.int32, st.shape, 0)
                qpos = qi * qb + g * gb + (lax.broadcasted_iota(jnp.int32, st.shape, 1) & (gb - 1))
                st = jnp.where(kpos <= qpos, st, NEG_BIG)
            _flash_update_t(st, ct_ref[...], m_ref, l_ref, acc_ref)

    crosses_diagonal = ki * tk + tk - 1 > qi * qb
    pl.when(crosses_diagonal)(functools.partial(update, True))
    pl.when(jnp.logical_not(crosses_diagonal))(functools.partial(update, False))

    @pl.when(ki == (qi * qb + qb - 1) // tk)
    def _():
        for g, (m_ref, l_ref, acc_ref) in enumerate(groups):
            o = (acc_ref[...] * (1.0 / l_ref[...])).T
            for hd in range(heads):
                o_ref[hd, g * gb:(g + 1) * gb, :] = o[hd * gb:(hd + 1) * gb, :].astype(o_ref.dtype)


def mla_attn_prompt(ql, qr, cb, krb, ct, *, nb, seq, qb=128, tk=512, nsplit=2):
    heads, m, kv_lora = ql.shape
    rope = qr.shape[2]
    qb, tk = min(qb, seq), min(tk, seq)
    gb = qb // nsplit
    assert gb & (gb - 1) == 0 and gb * nsplit == qb
    r = gb * heads
    nq, nk = seq // qb, seq // tk
    pairs = [(qi, ki) for qi in range(nq) for ki in range((qi * qb + qb - 1) // tk + 1)]
    qi_tab = jnp.asarray([p[0] for p in pairs], jnp.int32)
    ki_tab = jnp.asarray([p[1] for p in pairs], jnp.int32)
    qmap = lambda b, s, qt, kt: (0, b * nq + qt[s], 0)
    kern = functools.partial(_mla_attn_prompt_kernel, qb=qb, tk=tk, nsplit=nsplit)
    group_stats = [pltpu.VMEM((1, r), F32), pltpu.VMEM((1, r), F32), pltpu.VMEM((kv_lora, r), F32)]
    grid_spec = pltpu.PrefetchScalarGridSpec(
        num_scalar_prefetch=2,
        grid=(nb, len(pairs)),
        in_specs=[
            pl.BlockSpec((heads, qb, kv_lora), qmap),
            pl.BlockSpec((heads, qb, rope), qmap),
            pl.BlockSpec((tk, kv_lora), lambda b, s, qt, kt: (b * nk + kt[s], 0)),
            pl.BlockSpec((tk, rope), lambda b, s, qt, kt: (b * nk + kt[s], 0)),
            pl.BlockSpec((kv_lora, tk), lambda b, s, qt, kt: (0, b * nk + kt[s])),
        ],
        out_specs=pl.BlockSpec((heads, qb, kv_lora), qmap),
        scratch_shapes=group_stats * nsplit,
    )
    return pl.pallas_call(
        kern,
        grid_spec=grid_spec,
        out_shape=jax.ShapeDtypeStruct((heads, m, kv_lora), BF16),
        compiler_params=_cparams("parallel", "arbitrary"),
        name="mla_attn_prompt",
    )(qi_tab, ki_tab, ql, qr, cb, krb, ct)


def _mla_attn_sample_kernel(pt_ref, q_ref, qr_ref, cn_ref, krn_ref, lat_hbm, krt_hbm, o_alias,
                            o_ref, *scratch, layer, pp, page, seq, nchains):
    del o_alias
    chains = [scratch[3 * c:3 * c + 3] for c in range(nchains)]
    latbuf, krtbuf, kbuf_ref, krtb_ref, q2_ref, qr2_ref, lat_sem, krt_sem = scratch[3 * nchains:]
    b = pl.program_id(0)
    g = pl.program_id(1)
    ng = pl.num_programs(1)
    heads = q_ref.shape[0]
    t = b * ng + g
    last = pl.num_programs(0) * ng - 1
    slot = lax.rem(t, 2)
    nslot = 1 - slot
    tn = jnp.minimum(t + 1, last)
    bn = lax.div(tn, ng)
    gn = lax.rem(tn, ng)

    def lat_copy(bb, gg, i, s):
        pg = pt_ref[bb, gg * pp + i]
        return pltpu.make_async_copy(lat_hbm.at[layer, pg], latbuf.at[s, i], lat_sem.at[s])

    def krt_copy(bb, gg, i, s):
        pg = pt_ref[bb, gg * pp + i]
        return pltpu.make_async_copy(krt_hbm.at[layer, pg], krtbuf.at[s, i], krt_sem.at[s])

    @pl.when(t == 0)
    def _():
        for i in range(pp):
            lat_copy(b, g, i, 0).start()
            krt_copy(b, g, i, 0).start()

    @pl.when(g == 0)
    def _():
        for m_ref, l_ref, acc_ref in chains:
            _flash_init(m_ref, l_ref, acc_ref)
        q2_ref[...] = jnp.concatenate(
            [q_ref[hd].astype(F32) for hd in range(heads)], axis=0).astype(BF16)
        qr2_ref[...] = jnp.concatenate(
            [qr_ref[hd].astype(F32) for hd in range(heads)], axis=0).astype(BF16)

    for i in range(pp):
        lat_copy(b, g, i, slot).wait()
        krt_copy(b, g, i, slot).wait()
    for i in range(pp):
        lat_copy(bn, gn, i, nslot).start()
        krt_copy(bn, gn, i, nslot).start()
        kbuf_ref[i * page:(i + 1) * page, :] = latbuf[slot, i].astype(BF16)
        krtb_ref[:, i * page:(i + 1) * page] = krtbuf[slot, i].astype(BF16)
    q = q2_ref[...]
    qr = qr2_ref[...]
    part = pp * page // nchains
    for ch, (m_ref, l_ref, acc_ref) in enumerate(chains):
        kb = kbuf_ref[ch * part:(ch + 1) * part, :]
        s = _dot_nt(q, kb) + _dot(qr, krtb_ref[:, ch * part:(ch + 1) * part])
        _flash_update(s, kb, m_ref, l_ref, acc_ref)

    @pl.when(g == ng - 1)
    def _():
        cn = cn_ref[...].astype(BF16)
        s = _dot_nt(q, cn) + _dot_nt(qr, krn_ref[...].astype(BF16))
        qtok = lax.broadcasted_iota(jnp.int32, s.shape, 0) & (seq - 1)
        ktok = lax.broadcasted_iota(jnp.int32, s.shape, 1)
        _flash_update(jnp.where(ktok <= qtok, s, NEG_BIG), cn, *chains[0])
        mx = chains[0][0][...]
        for m_ref, _, _ in chains[1:]:
            mx = jnp.maximum(mx, m_ref[...])
        num = 0.0
        den = 0.0
        for m_ref, l_ref, acc_ref in chains:
            e = jnp.exp(m_ref[...] - mx)
            num = num + acc_ref[...] * e
            den = den + l_ref[...] * e
        o = num / den
        for hd in range(heads):
            o_ref[hd] = o[hd * seq:(hd + 1) * seq, :].astype(o_ref.dtype)

    @pl.when(t == last)
    def _():
        for i in range(pp):
            lat_copy(bn, gn, i, nslot).wait()
            krt_copy(bn, gn, i, nslot).wait()


def mla_attn_sample(page_table, ql, qr, c_new, kr_new, cache_lat, cache_krt, o_lat, layer,
                    *, row0, nb, seq, pp=32, nchains=1):
    heads, _, kv_lora = ql.shape
    rope = qr.shape[2]
    n_pages = page_table.shape[1]
    page = cache_lat.shape[2]
    pp = min(pp, n_pages)
    assert seq & (seq - 1) == 0 and pp % nchains == 0 and n_pages % pp == 0
    r = seq * heads
    blk0 = row0 // seq
    kern = functools.partial(_mla_attn_sample_kernel, layer=layer, pp=pp, page=page, seq=seq,
                             nchains=nchains)
    chain_stats = [pltpu.VMEM((r, 1), F32), pltpu.VMEM((r, 1), F32), pltpu.VMEM((r, kv_lora), F32)]
    grid_spec = pltpu.PrefetchScalarGridSpec(
        num_scalar_prefetch=1,
        grid=(nb, n_pages // pp),
        in_specs=[
            pl.BlockSpec((heads, seq, kv_lora), lambda b, g, pt: (0, blk0 + b, 0)),
            pl.BlockSpec((heads, seq, rope), lambda b, g, pt: (0, blk0 + b, 0)),
            pl.BlockSpec((seq, kv_lora), lambda b, g, pt: (blk0 + b, 0)),
            pl.BlockSpec((seq, rope), lambda b, g, pt: (blk0 + b, 0)),
            pl.BlockSpec(memory_space=pl.ANY),
            pl.BlockSpec(memory_space=pl.ANY),
            pl.BlockSpec(memory_space=pl.ANY),
        ],
        out_specs=pl.BlockSpec((heads, seq, kv_lora), lambda b, g, pt: (0, blk0 + b, 0)),
        scratch_shapes=[
            *(chain_stats * nchains),
            pltpu.VMEM((2, pp, page, kv_lora), F32), pltpu.VMEM((2, pp, rope, page), F32),
            pltpu.VMEM((pp * page, kv_lora), BF16), pltpu.VMEM((rope, pp * page), BF16),
            pltpu.VMEM((r, kv_lora), BF16), pltpu.VMEM((r, rope), BF16),
            pltpu.SemaphoreType.DMA((2,)), pltpu.SemaphoreType.DMA((2,)),
        ],
    )
    return pl.pallas_call(
        kern,
        grid_spec=grid_spec,
        out_shape=jax.ShapeDtypeStruct(o_lat.shape, o_lat.dtype),
        input_output_aliases={7: 0},
        compiler_params=_cparams("arbitrary", "arbitrary"),
        name="mla_attn_sample",
    )(page_table, ql, qr, c_new, kr_new, cache_lat, cache_krt, o_lat)


def _mla_post_kernel(ol_ref, wuv_ref, wo_ref, x_ref, o_ref, ob_ref, *, vdim):
    @pl.when(pl.program_id(1) == 0)
    def _():
        for hd in range(ol_ref.shape[0]):
            ob_ref[:, hd * vdim:(hd + 1) * vdim] = _dot(ol_ref[hd], wuv_ref[hd]).astype(BF16)

    o_ref[...] = x_ref[...] + _dot(ob_ref[...], wo_ref[...])


def mla_post(o_lat, wuv, wo, x, *, tm=256, tn=1024):
    m, d = x.shape
    heads, _, kv_lora = o_lat.shape
    vdim = wuv.shape[2]
    tm, tn = min(tm, m), min(tn, d)
    kern = functools.partial(_mla_post_kernel, vdim=vdim)
    return pl.pallas_call(
        kern,
        grid=(m // tm, d // tn),
        in_specs=[
            pl.BlockSpec((heads, tm, kv_lora), lambda i, j: (0, i, 0)),
            pl.BlockSpec(wuv.shape, lambda i, j: (0, 0, 0)),
            pl.BlockSpec((heads * vdim, tn), lambda i, j: (0, j)),
            pl.BlockSpec((tm, tn), lambda i, j: (i, j)),
        ],
        out_specs=pl.BlockSpec((tm, tn), lambda i, j: (i, j)),
        out_shape=jax.ShapeDtypeStruct((m, d), F32),
        scratch_shapes=[pltpu.VMEM((tm, heads * vdim), BF16)],
        compiler_params=_cparams("parallel", "arbitrary"),
        name="mla_post",
    )(o_lat, wuv, wo, x)


def _rot_half_cols(w, half):
    return jnp.concatenate([-w[..., half:], w[..., :half]], axis=-1)


def gla_layer(x, ln_g, w_in, w_gate_up, b_gate, g_norm, w_out, state_gla, states, layer, dims):
    nbp, lp, nbs, ls, heads, dk, dv, rank = dims
    nq, nv = heads * dk, heads * dv
    mp = nbp * lp
    g = ln_g.reshape(1, -1)
    w_qk = w_in[:, :2 * nq].astype(BF16)
    w_vr = w_in[:, 2 * nq:2 * nq + 2 * nv].astype(BF16)
    lane = 128
    wa = jnp.pad(w_in[:, 2 * nq + 2 * nv:], ((0, 0), (0, lane - rank))).astype(BF16)
    wg = jnp.pad(w_gate_up, ((0, lane - rank), (0, 0))).astype(BF16)
    qk = norm_matmul(x, g, w_qk, F32)
    vr = norm_matmul(x, g, w_vr, BF16)
    la = gla_gate(x, g, wa, wg, b_gate.reshape(1, -1))
    gn = g_norm.reshape(1, -1)
    sp_all, ss_all = states
    og, sp_all = gla_prompt(qk, vr, la, gn, sp_all, layer, nb=nbp, seq=lp, heads=heads, dk=dk, dv=dv,
                            n_layers=state_gla.shape[0])
    og, ss_all = gla_sample(qk, vr, la, gn, state_gla, og, ss_all, layer, row0=mp, nb=nbs, seq=ls,
                            heads=heads, dk=dk, dv=dv)
    x = matmul_res(og, w_out.astype(BF16), x)
    return x, (sp_all, ss_all)


def mla_layer(x, ln_g, w_in, g_q, w_qb, g_kv, w_uk, w_uv, w_out, cache_lat, cache_krt, page_table,
              cos, sin, layer, dims):
    nbp, lp, nbs, ls, heads, q_lora, kv_lora, nope, rope = dims
    mp = nbp * lp
    half = rope // 2
    o = q_lora + kv_lora
    win_ext = jnp.concatenate([w_in, _rot_half_cols(w_in[:, o:o + rope], half)], axis=1).astype(BF16)
    wqb = w_qb.reshape(q_lora, heads, nope + rope)
    w_rope = wqb[:, :, nope:]
    wqb_ext = jnp.concatenate([
        wqb[:, :, :nope].reshape(q_lora, heads * nope),
        w_rope.reshape(q_lora, heads * rope),
        _rot_half_cols(w_rope, half).reshape(q_lora, heads * rope)], axis=1).astype(BF16)
    wuk_t = jnp.transpose(w_uk, (1, 2, 0)).astype(BF16)
    wuv = jnp.transpose(w_uv, (1, 0, 2)).astype(BF16)
    c_new, kr_new, cb, ct, krb, ql, qr = mla_pre(
        x, ln_g.reshape(1, -1), win_ext, g_q.reshape(1, -1), g_kv.reshape(1, -1), wqb_ext, wuk_t,
        cos, sin, heads=heads, q_lora=q_lora, kv_lora=kv_lora, nope=nope, rope=rope)
    o_lat = mla_attn_prompt(ql, qr, cb, krb, ct, nb=nbp, seq=lp)
    o_lat = mla_attn_sample(page_table, ql, qr, c_new, kr_new, cache_lat, cache_krt, o_lat, layer,
                            row0=mp, nb=nbs, seq=ls)
    x = mla_post(o_lat, wuv, w_out.astype(BF16), x)
    return x, c_new, kr_new


def kernel(x_prompt, x_sample, state_gla, cache_mla_latent, cache_mla_krope, page_table, ln_mixer_g, ln_ffn_g, gla_w_in, gla_w_gate_up, gla_b_gate, gla_norm_g, gla_w_out, mla_w_in, mla_q_norm_g, mla_w_qb, mla_kv_norm_g, mla_w_uk, mla_w_uv, mla_w_out, ffn_w_gate, ffn_w_up, ffn_w_down, final_norm_g):
    nbp, lp, d = x_prompt.shape
    nbs, ls, _ = x_sample.shape
    mp, ms = nbp * lp, nbs * ls
    depth = ln_mixer_g.shape[0]
    heads_g, dk, dv = state_gla.shape[2:]
    rank = gla_w_gate_up.shape[1]
    kv_lora, heads_m, nope = mla_w_uk.shape[1:]
    q_lora = mla_q_norm_g.shape[1]
    rope = cache_mla_krope.shape[3]
    past_len = page_table.shape[1] * cache_mla_latent.shape[2]

    x = jnp.concatenate([x_prompt.reshape(mp, d), x_sample.reshape(ms, d)], axis=0)

    half = rope // 2
    inv = 1.0 / (ROPE_THETA ** (jnp.arange(half, dtype=F32) / half))
    pos = jnp.concatenate([jnp.tile(jnp.arange(lp, dtype=jnp.int32), nbp),
                           jnp.tile(past_len + jnp.arange(ls, dtype=jnp.int32), nbs)])
    ang = pos.astype(F32)[:, None] * inv[None, :]
    cos = jnp.tile(jnp.cos(ang), (1, 2 * rope // half))
    sin = jnp.tile(jnp.sin(ang), (1, 2 * rope // half))

    gla_dims = (nbp, lp, nbs, ls, heads_g, dk, dv, rank)
    mla_dims = (nbp, lp, nbs, ls, heads_m, q_lora, kv_lora, nope, rope)
    cache_krt = jnp.swapaxes(cache_mla_krope, 2, 3)
    gla_states = (None, None)
    mla_c, mla_kr = [], []
    for i in range(depth):
        j = i // 2
        if i % 2 == 0:
            x, gla_states = gla_layer(x, ln_mixer_g[i], gla_w_in[j], gla_w_gate_up[j], gla_b_gate[j],
                                      gla_norm_g[j], gla_w_out[j], state_gla, gla_states, j, gla_dims)
        else:
            x, c_new, kr_new = mla_layer(x, ln_mixer_g[i], mla_w_in[j], mla_q_norm_g[j], mla_w_qb[j],
                                         mla_kv_norm_g[j], mla_w_uk[j], mla_w_uv[j], mla_w_out[j],
                                         cache_mla_latent, cache_krt, page_table, cos, sin, j,
                                         mla_dims)
            mla_c.append(c_new)
            mla_kr.append(kr_new)
        x = ffn(x, ln_ffn_g[i].reshape(1, -1), ffn_w_gate[i].astype(BF16), ffn_w_up[i].astype(BF16),
                ffn_w_down[i].astype(BF16))

    fg = final_norm_g.reshape(1, -1)
    y_prompt = rmsnorm_rows(x, fg, 0, mp).reshape(nbp, lp, d)
    y_sample = rmsnorm_rows(x, fg, mp, ms).reshape(nbs, ls, d)
    c_all = jnp.stack(mla_c)
    kr_all = jnp.stack(mla_kr)
    return (y_prompt, y_sample, gla_states[0], gla_states[1],
            c_all[:, :mp].reshape(-1, nbp, lp, kv_lora), kr_all[:, :mp].reshape(-1, nbp, lp, rope),
            c_all[:, mp:].reshape(-1, nbs, ls, kv_lora), kr_all[:, mp:].reshape(-1, nbs, ls, rope))
```

```python
import functools

import jax
import jax.numpy as jnp
from jax import lax
from jax.experimental import pallas as pl
from jax.experimental.pallas import tpu as pltpu

F32 = jnp.float32
BF16 = jnp.bfloat16

RMS_EPS = 1e-6
ROPE_THETA = 10000.0
GLA_GATE_TEMP = 16.0
GLA_CHUNK = 64
GLA_SUB = 16
EXP_CLAMP = 80.0
NEG_BIG = -1e30

V7X_VMEM_BYTES = 64 * 1024 * 1024
VMEM_LIMIT_BYTES = V7X_VMEM_BYTES - 8 * 1024 * 1024


def _cparams(*sem):
    return pltpu.CompilerParams(dimension_semantics=sem, vmem_limit_bytes=VMEM_LIMIT_BYTES)


def _rms(x, g):
    ms = jnp.mean(x * x, axis=-1, keepdims=True)
    return x * lax.rsqrt(ms + RMS_EPS) * g


def _silu(x):
    return x * jax.nn.sigmoid(x)


def _dot(a, b):
    return jnp.dot(a, b, preferred_element_type=F32)


def _dot_nt(a, b):
    return lax.dot_general(a, b, (((1,), (1,)), ((), ())), preferred_element_type=F32)


def _dot_tn(a, b):
    return lax.dot_general(a, b, (((0,), (0,)), ((), ())), preferred_element_type=F32)


def _norm_matmul_kernel(x_ref, g_ref, w_ref, o_ref, h_ref):
    @pl.when(pl.program_id(1) == 0)
    def _():
        h_ref[...] = _rms(x_ref[...], g_ref[...]).astype(BF16)

    o_ref[...] = _dot(h_ref[...], w_ref[...]).astype(o_ref.dtype)


def norm_matmul(x, g, w, out_dtype, *, tm=512, tn=1024):
    m, d = x.shape
    n = w.shape[1]
    tm, tn = min(tm, m), min(tn, n)
    return pl.pallas_call(
        _norm_matmul_kernel,
        grid=(m // tm, n // tn),
        in_specs=[
            pl.BlockSpec((tm, d), lambda i, j: (i, 0)),
            pl.BlockSpec((1, d), lambda i, j: (0, 0)),
            pl.BlockSpec((d, tn), lambda i, j: (0, j)),
        ],
        out_specs=pl.BlockSpec((tm, tn), lambda i, j: (i, j)),
        out_shape=jax.ShapeDtypeStruct((m, n), out_dtype),
        scratch_shapes=[pltpu.VMEM((tm, d), BF16)],
        compiler_params=_cparams("parallel", "arbitrary"),
        name="norm_matmul",
    )(x, g, w)


def _norm_matmul_t_kernel(x_ref, g_ref, wt_ref, o_ref, wb_ref):
    @pl.when(pl.program_id(1) == 0)
    def _():
        wb_ref[...] = wt_ref[...].astype(BF16)

    h = _rms(x_ref[...], g_ref[...]).astype(BF16)
    o_ref[...] = _dot_nt(h, wb_ref[...]).astype(o_ref.dtype)


def norm_matmul_t(x, g, wt, layer, row0, n, out_dtype, *, tm=512, tn=1024):
    m, d = x.shape
    tm, tn = min(tm, m), min(tn, n)
    assert row0 % tn == 0 and n % tn == 0 and m % tm == 0
    blk0 = row0 // tn
    return pl.pallas_call(
        _norm_matmul_t_kernel,
        grid=(n // tn, m // tm),
        in_specs=[
            pl.BlockSpec((tm, d), lambda j, i: (i, 0)),
            pl.BlockSpec((1, d), lambda j, i: (0, 0)),
            pl.BlockSpec((None, tn, d), lambda j, i: (layer, blk0 + j, 0)),
        ],
        out_specs=pl.BlockSpec((tm, tn), lambda j, i: (i, j)),
        out_shape=jax.ShapeDtypeStruct((m, n), out_dtype),
        scratch_shapes=[pltpu.VMEM((tn, d), BF16)],
        compiler_params=_cparams("parallel", "arbitrary"),
        name="norm_matmul_t",
    )(x, g, wt)


def _matmul_res_kernel(a_ref, w_ref, r_ref, o_ref):
    o_ref[...] = r_ref[...] + _dot(a_ref[...], w_ref[...])


def matmul_res(a, w, res, *, tm=512, tn=1024):
    m, k = a.shape
    n = w.shape[1]
    tm, tn = min(tm, m), min(tn, n)
    return pl.pallas_call(
        _matmul_res_kernel,
        grid=(m // tm, n // tn),
        in_specs=[
            pl.BlockSpec((tm, k), lambda i, j: (i, 0)),
            pl.BlockSpec((k, tn), lambda i, j: (0, j)),
            pl.BlockSpec((tm, tn), lambda i, j: (i, j)),
        ],
        out_specs=pl.BlockSpec((tm, tn), lambda i, j: (i, j)),
        out_shape=jax.ShapeDtypeStruct((m, n), F32),
        compiler_params=_cparams("parallel", "arbitrary"),
        name="matmul_res",
    )(a, w, res)


def _ffn_kernel(x_ref, g_ref, wg_ref, wu_ref, wd_ref, o_ref, h_ref):
    @pl.when(pl.program_id(1) == 0)
    def _():
        x = x_ref[...]
        h_ref[...] = _rms(x, g_ref[...]).astype(BF16)
        o_ref[...] = x

    h = h_ref[...]
    gate = _dot(h, wg_ref[...].astype(BF16))
    up = _dot(h, wu_ref[...].astype(BF16))
    a = (_silu(gate) * up).astype(BF16)
    o_ref[...] += _dot(a, wd_ref[...].astype(BF16))


def ffn(x, g, wg, wu, wd, layer, *, tm=1024, tf=256):
    m, d = x.shape
    f = wg.shape[2]
    tm, tf = min(tm, m), min(tf, f)
    while m % tm:
        tm //= 2
    assert f % tf == 0
    return pl.pallas_call(
        _ffn_kernel,
        grid=(m // tm, f // tf),
        in_specs=[
            pl.BlockSpec((tm, d), lambda i, j: (i, 0)),
            pl.BlockSpec((1, d), lambda i, j: (0, 0)),
            pl.BlockSpec((None, d, tf), lambda i, j: (layer, 0, j)),
            pl.BlockSpec((None, d, tf), lambda i, j: (layer, 0, j)),
            pl.BlockSpec((None, tf, d), lambda i, j: (layer, j, 0)),
        ],
        out_specs=pl.BlockSpec((tm, d), lambda i, j: (i, 0)),
        out_shape=jax.ShapeDtypeStruct((m, d), F32),
        scratch_shapes=[pltpu.VMEM((tm, d), BF16)],
        compiler_params=_cparams("parallel", "arbitrary"),
        name="ffn",
    )(x, g, wg, wu, wd)


def _rmsnorm_kernel(x_ref, g_ref, o_ref):
    o_ref[...] = _rms(x_ref[...], g_ref[...])


def rmsnorm_rows(x, g, row0, nrows, *, tm=512):
    d = x.shape[1]
    tm = min(tm, nrows)
    blk0 = row0 // tm
    return pl.pallas_call(
        _rmsnorm_kernel,
        grid=(nrows // tm,),
        in_specs=[
            pl.BlockSpec((tm, d), lambda i: (blk0 + i, 0)),
            pl.BlockSpec((1, d), lambda i: (0, 0)),
        ],
        out_specs=pl.BlockSpec((tm, d), lambda i: (i, 0)),
        out_shape=jax.ShapeDtypeStruct((nrows, d), F32),
        compiler_params=_cparams("parallel"),
        name="final_rmsnorm",
    )(x, g)


def _gla_gate_kernel(x_ref, g_ref, wa_ref, wg_ref, bg_ref, o_ref):
    h = _rms(x_ref[...], g_ref[...]).astype(BF16)
    a = _dot_nt(h, wa_ref[...]).astype(BF16)
    z = _dot(a, wg_ref[...]) + bg_ref[...]
    log_sig = jnp.minimum(z, 0.0) - jnp.log1p(jnp.exp(-jnp.abs(z)))
    o_ref[...] = log_sig * (1.0 / GLA_GATE_TEMP)


def gla_gate(x, g, wa, wg, bg, *, tm=512):
    m, d = x.shape
    n = wg.shape[1]
    tm = min(tm, m)
    return pl.pallas_call(
        _gla_gate_kernel,
        grid=(m // tm,),
        in_specs=[
            pl.BlockSpec((tm, d), lambda i: (i, 0)),
            pl.BlockSpec((1, d), lambda i: (0, 0)),
            pl.BlockSpec(wa.shape, lambda i: (0, 0)),
            pl.BlockSpec(wg.shape, lambda i: (0, 0)),
            pl.BlockSpec((1, n), lambda i: (0, 0)),
        ],
        out_specs=pl.BlockSpec((tm, n), lambda i: (i, 0)),
        out_shape=jax.ShapeDtypeStruct((m, n), F32),
        compiler_params=_cparams("parallel"),
        name="gla_gate",
    )(x, g, wa, wg, bg)


def _cumsum_rows(la):
    c = la.shape[0]
    if c <= 8:
        rows = lax.broadcasted_iota(jnp.int32, la.shape, 0)
        out = jnp.zeros_like(la)
        for j in range(c):
            out = out + jnp.where(rows >= j, la[j:j + 1, :], 0.0)
        return out
    r = lax.broadcasted_iota(jnp.int32, (c, c), 0)
    s = lax.broadcasted_iota(jnp.int32, (c, c), 1)
    tri = jnp.where(s <= r, 1.0, 0.0).astype(BF16)
    hi = la.astype(BF16)
    lo = (la - hi.astype(F32)).astype(BF16)
    return _dot(tri, hi) + _dot(tri, lo)


def _gla_chunk(qs, k, v, la, s, sub):
    c, dk = qs.shape
    b = _cumsum_rows(la)
    b_last = b[c - 1:c, :]
    o = _dot((qs * jnp.exp(b)).astype(BF16), s.astype(BF16))
    col = lax.broadcasted_iota(jnp.int32, (sub, c), 1)
    row = lax.broadcasted_iota(jnp.int32, (sub, c), 0)
    parts = []
    for i in range(c // sub):
        lo = i * sub
        ref = b[lo - 1:lo, :] if i else jnp.zeros((1, dk), F32)
        qi = (qs[lo:lo + sub, :] * jnp.exp(b[lo:lo + sub, :] - ref)).astype(BF16)
        kd = (k * jnp.exp(jnp.minimum(ref - b, EXP_CLAMP))).astype(BF16)
        parts.append(jnp.where(col <= row + lo, _dot_nt(qi, kd), 0.0))
    att = parts[0] if len(parts) == 1 else jnp.concatenate(parts, axis=0)
    o = o + _dot(att.astype(BF16), v)
    kdl = (k * jnp.exp(b_last - b)).astype(BF16)
    decay_col = jnp.exp(b[c - 8:c, :].T[:, 7:8])
    s_new = decay_col * s + _dot_tn(kdl, v)
    return o, s_new


def _gla_gate_out(o, r, gn):
    return (_rms(o, gn) * _silu(r.astype(F32))).astype(BF16)


def _gla_prompt_kernel(q_ref, k_ref, v_ref, r_ref, la_ref, gn_ref, o_ref, s_ref,
                       *, chunk, sub, scale, unroll):
    @pl.when(pl.program_id(2) == 0)
    def _():
        s_ref[...] = jnp.zeros_like(s_ref)

    hg, dk, dv = s_ref.shape

    def body(c, carry):
        rows = pl.ds(pl.multiple_of(c * chunk, chunk), chunk)
        for j in range(hg):
            kc = slice(j * dk, (j + 1) * dk)
            vc = slice(j * dv, (j + 1) * dv)
            o, s_new = _gla_chunk(q_ref[rows, kc] * scale, k_ref[rows, kc], v_ref[rows, vc],
                                  la_ref[rows, kc], s_ref[j], sub)
            s_ref[j] = s_new
            o_ref[rows, vc] = _gla_gate_out(o, r_ref[rows, vc], gn_ref[...])
        return carry

    lax.fori_loop(0, q_ref.shape[0] // chunk, body, 0, unroll=unroll)


def gla_prompt(qk, vr, la, gn, s_all, layer, *, nb, seq, heads, dk, dv, n_layers, tb=512, hg=4,
               unroll=1):
    m = qk.shape[0]
    chunk = min(GLA_CHUNK, seq)
    tb = min(tb, seq)
    nt = seq // tb
    ngrp = heads // hg
    assert ngrp * hg == heads
    row = lambda b, h, t: b * nt + t
    in_specs = [
        pl.BlockSpec((tb, hg * dk), lambda b, h, t: (row(b, h, t), h)),
        pl.BlockSpec((tb, hg * dk), lambda b, h, t: (row(b, h, t), ngrp + h)),
        pl.BlockSpec((tb, hg * dv), lambda b, h, t: (row(b, h, t), h)),
        pl.BlockSpec((tb, hg * dv), lambda b, h, t: (row(b, h, t), ngrp + h)),
        pl.BlockSpec((tb, hg * dk), lambda b, h, t: (row(b, h, t), h)),
        pl.BlockSpec((1, dv), lambda b, h, t: (0, 0)),
    ]
    args = [qk, qk, vr, vr, la, gn]
    aliases = {}
    if s_all is not None:
        in_specs.append(pl.BlockSpec(memory_space=pl.ANY))
        args.append(s_all)
        aliases = {len(args) - 1: 1}
    kern = functools.partial(_gla_prompt_kernel, chunk=chunk, sub=min(GLA_SUB, chunk), scale=dk ** -0.5,
                             unroll=unroll)
    if s_all is not None:
        kern = functools.partial(_drop_arg, kern, 6)
    return pl.pallas_call(
        kern,
        grid=(nb, ngrp, nt),
        in_specs=in_specs,
        out_specs=[
            pl.BlockSpec((tb, hg * dv), lambda b, h, t: (row(b, h, t), h)),
            pl.BlockSpec((None, None, hg, dk, dv), lambda b, h, t: (layer, b, h, 0, 0)),
        ],
        out_shape=[
            jax.ShapeDtypeStruct((m, heads * dv), BF16),
            jax.ShapeDtypeStruct((n_layers, nb, heads, dk, dv), F32),
        ],
        input_output_aliases=aliases,
        compiler_params=_cparams("parallel", "parallel", "arbitrary"),
        name="gla_prompt",
    )(*args)


def _drop_arg(fn, idx, *refs):
    return fn(*refs[:idx], *refs[idx + 1:])


def _drop_args(fn, idxs, *refs):
    return fn(*[r for i, r in enumerate(refs) if i not in idxs])


def _gla_sample_kernel(q_ref, k_ref, v_ref, r_ref, la_ref, gn_ref, s0_ref, o_ref, s_ref,
                       *, seq, gb, heads, dk, dv, scale):
    for g in range(gb):
        rows = slice(g * seq, (g + 1) * seq)
        for h in range(heads):
            kc = slice(h * dk, (h + 1) * dk)
            vc = slice(h * dv, (h + 1) * dv)
            o, s_new = _gla_chunk(q_ref[rows, kc] * scale, k_ref[rows, kc], v_ref[rows, vc],
                                  la_ref[rows, kc], s0_ref[g, h], seq)
            s_ref[g, h] = s_new
            o_ref[rows, vc] = _gla_gate_out(o, r_ref[rows, vc], gn_ref[...])


def gla_sample(qk, vr, la, gn, state, og, s_all, layer, *, row0, nb, seq, heads, dk, dv, gb=2):
    m = qk.shape[0]
    n_layers = state.shape[0]
    tb = gb * seq
    blk0 = row0 // tb
    in_specs = [
        pl.BlockSpec((tb, heads * dk), lambda i: (blk0 + i, 0)),
        pl.BlockSpec((tb, heads * dk), lambda i: (blk0 + i, 1)),
        pl.BlockSpec((tb, heads * dv), lambda i: (blk0 + i, 0)),
        pl.BlockSpec((tb, heads * dv), lambda i: (blk0 + i, 1)),
        pl.BlockSpec((tb, heads * dk), lambda i: (blk0 + i, 0)),
        pl.BlockSpec((1, dv), lambda i: (0, 0)),
        pl.BlockSpec((None, gb, heads, dk, dv), lambda i: (layer, i, 0, 0, 0)),
        pl.BlockSpec(memory_space=pl.ANY),
    ]
    args = [qk, qk, vr, vr, la, gn, state, og]
    aliases = {7: 0}
    drop = [7]
    if s_all is not None:
        in_specs.append(pl.BlockSpec(memory_space=pl.ANY))
        args.append(s_all)
        aliases[8] = 1
        drop.append(8)
    kern = functools.partial(_gla_sample_kernel, seq=seq, gb=gb, heads=heads, dk=dk, dv=dv,
                             scale=dk ** -0.5)
    kern = functools.partial(_drop_args, kern, tuple(drop))
    return pl.pallas_call(
        kern,
        grid=(nb // gb,),
        in_specs=in_specs,
        out_specs=[
            pl.BlockSpec((tb, heads * dv), lambda i: (blk0 + i, 0)),
            pl.BlockSpec((None, gb, heads, dk, dv), lambda i: (layer, i, 0, 0, 0)),
        ],
        out_shape=[
            jax.ShapeDtypeStruct((m, heads * dv), BF16),
            jax.ShapeDtypeStruct((n_layers, nb, heads, dk, dv), F32),
        ],
        input_output_aliases=aliases,
        compiler_params=_cparams("parallel"),
        name="gla_sample",
    )(*args)


def _mla_pre_kernel(x_ref, g_ref, win_ref, gq_ref, gkv_ref, wqb_ref, wuk_ref, cos_ref, sin_ref,
                    c_ref, kr_ref, cb_ref, ct_ref, krb_ref, ql_ref, qr_ref,
                    *, heads, q_lora, kv_lora, nope, rope, scale):
    h = _rms(x_ref[...], g_ref[...]).astype(BF16)
    p = _dot_nt(h, win_ref[...])
    cos = cos_ref[...]
    sin = sin_ref[...]
    c = _rms(p[:, q_lora:q_lora + kv_lora], gkv_ref[...])
    c_ref[...] = c
    cb_ref[...] = c.astype(BF16)
    ct_ref[...] = c.T.astype(BF16)
    o = q_lora + kv_lora
    kr = p[:, o:o + rope] * cos[:, :rope] + p[:, o + rope:o + 2 * rope] * sin[:, :rope]
    kr_ref[...] = kr
    krb_ref[...] = kr.astype(BF16)
    cqn = _rms(p[:, :q_lora], gq_ref[...]).astype(BF16)
    q = _dot(cqn, wqb_ref[...])
    n0 = heads * nope
    n1 = n0 + heads * rope
    pair = 2 * rope
    for hp in range(heads * rope // pair):
        a = q[:, n0 + hp * pair:n0 + (hp + 1) * pair]
        b = q[:, n1 + hp * pair:n1 + (hp + 1) * pair]
        two = ((a * cos + b * sin) * scale).astype(BF16)
        qr_ref[2 * hp] = two[:, :rope]
        qr_ref[2 * hp + 1] = two[:, rope:]
    for hd in range(heads):
        qn = q[:, hd * nope:(hd + 1) * nope].astype(BF16)
        ql_ref[hd] = (_dot(qn, wuk_ref[hd]) * scale).astype(BF16)


def mla_pre(x, g, win_ext, gq, gkv, wqb_ext, wuk_t, cos, sin, *, heads, q_lora, kv_lora, nope, rope,
            tm=256):
    m, d = x.shape
    tm = min(tm, m)
    full = lambda a: pl.BlockSpec(a.shape, lambda i: (0,) * a.ndim)
    rows = lambda n: pl.BlockSpec((tm, n), lambda i: (i, 0))
    hrows = lambda n: pl.BlockSpec((heads, tm, n), lambda i: (0, i, 0))
    kern = functools.partial(_mla_pre_kernel, heads=heads, q_lora=q_lora, kv_lora=kv_lora, nope=nope,
                             rope=rope, scale=(nope + rope) ** -0.5)
    return pl.pallas_call(
        kern,
        grid=(m // tm,),
        in_specs=[rows(d), full(g), full(win_ext), full(gq), full(gkv), full(wqb_ext), full(wuk_t),
                  rows(cos.shape[1]), rows(sin.shape[1])],
        out_specs=[rows(kv_lora), rows(rope), rows(kv_lora),
                   pl.BlockSpec((kv_lora, tm), lambda i: (0, i)), rows(rope),
                   hrows(kv_lora), hrows(rope)],
        out_shape=[
            jax.ShapeDtypeStruct((m, kv_lora), F32),
            jax.ShapeDtypeStruct((m, rope), F32),
            jax.ShapeDtypeStruct((m, kv_lora), BF16),
            jax.ShapeDtypeStruct((kv_lora, m), BF16),
            jax.ShapeDtypeStruct((m, rope), BF16),
            jax.ShapeDtypeStruct((heads, m, kv_lora), BF16),
            jax.ShapeDtypeStruct((heads, m, rope), BF16),
        ],
        compiler_params=_cparams("parallel"),
        name="mla_pre",
    )(x, g, win_ext, gq, gkv, wqb_ext, wuk_t, cos, sin)


def _flash_update(s, v, m_ref, l_ref, acc_ref):
    m_prev = m_ref[...]
    m_new = jnp.maximum(m_prev, jnp.max(s, axis=-1, keepdims=True))
    alpha = jnp.exp(m_prev - m_new)
    p = jnp.exp(s - m_new)
    l_ref[...] = alpha * l_ref[...] + jnp.sum(p, axis=-1, keepdims=True)
    acc_ref[...] = alpha * acc_ref[...] + _dot(p.astype(BF16), v)
    m_ref[...] = m_new


def _flash_update_t(st, vt, m_ref, l_ref, acc_ref):
    m_prev = m_ref[...]
    m_new = jnp.maximum(m_prev, jnp.max(st, axis=0, keepdims=True))
    alpha = jnp.exp(m_prev - m_new)
    p = jnp.exp(st - m_new)
    l_ref[...] = alpha * l_ref[...] + jnp.sum(p, axis=0, keepdims=True)
    acc_ref[...] = alpha * acc_ref[...] + _dot(vt, p.astype(BF16))
    m_ref[...] = m_new


def _flash_init(m_ref, l_ref, acc_ref):
    m_ref[...] = jnp.full_like(m_ref, NEG_BIG)
    l_ref[...] = jnp.zeros_like(l_ref)
    acc_ref[...] = jnp.zeros_like(acc_ref)


def _mla_attn_prompt_kernel(qi_ref, ki_ref, q_ref, qr_ref, c_ref, kr_ref, ct_ref, o_ref, *stats,
                            qb, tk, nsplit):
    step = pl.program_id(1)
    qi = qi_ref[step]
    ki = ki_ref[step]
    heads = q_ref.shape[0]
    gb = qb // nsplit
    r = heads * gb
    groups = [stats[3 * g:3 * g + 3] for g in range(nsplit)]

    @pl.when(ki == 0)
    def _():
        for m_ref, l_ref, acc_ref in groups:
            _flash_init(m_ref, l_ref, acc_ref)

    def update(masked):
        for g, (m_ref, l_ref, acc_ref) in enumerate(groups):
            q = q_ref[:, g * gb:(g + 1) * gb, :].reshape(r, q_ref.shape[2])
            qr = qr_ref[:, g * gb:(g + 1) * gb, :].reshape(r, qr_ref.shape[2])
            st = _dot_nt(c_ref[...], q) + _dot_nt(kr_ref[...], qr)
            if masked:
                kpos = ki * tk + lax.broadcasted_iota(jnp.int32, st.shape, 0)
                qpos = qi * qb + g * gb + (lax.broadcasted_iota(jnp.int32, st.shape, 1) & (gb - 1))
                st = jnp.where(kpos <= qpos, st, NEG_BIG)
            _flash_update_t(st, ct_ref[...], m_ref, l_ref, acc_ref)

    crosses_diagonal = ki * tk + tk - 1 > qi * qb
    pl.when(crosses_diagonal)(functools.partial(update, True))
    pl.when(jnp.logical_not(crosses_diagonal))(functools.partial(update, False))

    @pl.when(ki == (qi * qb + qb - 1) // tk)
    def _():
        for g, (m_ref, l_ref, acc_ref) in enumerate(groups):
            o = (acc_ref[...] * (1.0 / l_ref[...])).T
            for hd in range(heads):
                o_ref[hd, g * gb:(g + 1) * gb, :] = o[hd * gb:(hd + 1) * gb, :].astype(o_ref.dtype)


def mla_attn_prompt(ql, qr, cb, krb, ct, *, nb, seq, qb=128, tk=512, nsplit=2):
    heads, m, kv_lora = ql.shape
    rope = qr.shape[2]
    qb, tk = min(qb, seq), min(tk, seq)
    gb = qb // nsplit
    assert gb & (gb - 1) == 0 and gb * nsplit == qb
    r = gb * heads
    nq, nk = seq // qb, seq // tk
    pairs = [(qi, ki) for qi in range(nq) for ki in range((qi * qb + qb - 1) // tk + 1)]
    qi_tab = jnp.asarray([p[0] for p in pairs], jnp.int32)
    ki_tab = jnp.asarray([p[1] for p in pairs], jnp.int32)
    qmap = lambda b, s, qt, kt: (0, b * nq + qt[s], 0)
    kern = functools.partial(_mla_attn_prompt_kernel, qb=qb, tk=tk, nsplit=nsplit)
    group_stats = [pltpu.VMEM((1, r), F32), pltpu.VMEM((1, r), F32), pltpu.VMEM((kv_lora, r), F32)]
    grid_spec = pltpu.PrefetchScalarGridSpec(
        num_scalar_prefetch=2,
        grid=(nb, len(pairs)),
        in_specs=[
            pl.BlockSpec((heads, qb, kv_lora), qmap),
            pl.BlockSpec((heads, qb, rope), qmap),
            pl.BlockSpec((tk, kv_lora), lambda b, s, qt, kt: (b * nk + kt[s], 0)),
            pl.BlockSpec((tk, rope), lambda b, s, qt, kt: (b * nk + kt[s], 0)),
            pl.BlockSpec((kv_lora, tk), lambda b, s, qt, kt: (0, b * nk + kt[s])),
        ],
        out_specs=pl.BlockSpec((heads, qb, kv_lora), qmap),
        scratch_shapes=group_stats * nsplit,
    )
    return pl.pallas_call(
        kern,
        grid_spec=grid_spec,
        out_shape=jax.ShapeDtypeStruct((heads, m, kv_lora), BF16),
        compiler_params=_cparams("parallel", "arbitrary"),
        name="mla_attn_prompt",
    )(qi_tab, ki_tab, ql, qr, cb, krb, ct)


def _mla_attn_sample_kernel(pt_ref, q_ref, qr_ref, cn_ref, krn_ref, lat_hbm, krt_hbm, o_alias,
                            o_ref, *scratch, layer, pp, page, seq, nchains):
    del o_alias
    chains = [scratch[3 * c:3 * c + 3] for c in range(nchains)]
    latbuf, krtbuf, kbuf_ref, krtb_ref, q2_ref, qr2_ref, lat_sem, krt_sem = scratch[3 * nchains:]
    b = pl.program_id(0)
    g = pl.program_id(1)
    ng = pl.num_programs(1)
    heads = q_ref.shape[0]
    t = b * ng + g
    last = pl.num_programs(0) * ng - 1
    slot = lax.rem(t, 2)
    nslot = 1 - slot
    tn = jnp.minimum(t + 1, last)
    bn = lax.div(tn, ng)
    gn = lax.rem(tn, ng)

    def lat_copy(bb, gg, i, s):
        pg = pt_ref[bb, gg * pp + i]
        return pltpu.make_async_copy(lat_hbm.at[layer, pg], latbuf.at[s, i], lat_sem.at[s])

    def krt_copy(bb, gg, i, s):
        pg = pt_ref[bb, gg * pp + i]
        return pltpu.make_async_copy(krt_hbm.at[layer, pg], krtbuf.at[s, i], krt_sem.at[s])

    @pl.when(t == 0)
    def _():
        for i in range(pp):
            lat_copy(b, g, i, 0).start()
            krt_copy(b, g, i, 0).start()

    @pl.when(g == 0)
    def _():
        for m_ref, l_ref, acc_ref in chains:
            _flash_init(m_ref, l_ref, acc_ref)
        q2_ref[...] = jnp.concatenate(
            [q_ref[hd].astype(F32) for hd in range(heads)], axis=0).astype(BF16)
        qr2_ref[...] = jnp.concatenate(
            [qr_ref[hd].astype(F32) for hd in range(heads)], axis=0).astype(BF16)

    for i in range(pp):
        lat_copy(b, g, i, slot).wait()
        krt_copy(b, g, i, slot).wait()
    for i in range(pp):
        lat_copy(bn, gn, i, nslot).start()
        krt_copy(bn, gn, i, nslot).start()
        kbuf_ref[i * page:(i + 1) * page, :] = latbuf[slot, i].astype(BF16)
        krtb_ref[:, i * page:(i + 1) * page] = krtbuf[slot, i].astype(BF16)
    q = q2_ref[...]
    qr = qr2_ref[...]
    part = pp * page // nchains
    for ch, (m_ref, l_ref, acc_ref) in enumerate(chains):
        kb = kbuf_ref[ch * part:(ch + 1) * part, :]
        s = _dot_nt(q, kb) + _dot(qr, krtb_ref[:, ch * part:(ch + 1) * part])
        _flash_update(s, kb, m_ref, l_ref, acc_ref)

    @pl.when(g == ng - 1)
    def _():
        cn = cn_ref[...].astype(BF16)
        s = _dot_nt(q, cn) + _dot_nt(qr, krn_ref[...].astype(BF16))
        qtok = lax.broadcasted_iota(jnp.int32, s.shape, 0) & (seq - 1)
        ktok = lax.broadcasted_iota(jnp.int32, s.shape, 1)
        _flash_update(jnp.where(ktok <= qtok, s, NEG_BIG), cn, *chains[0])
        mx = chains[0][0][...]
        for m_ref, _, _ in chains[1:]:
            mx = jnp.maximum(mx, m_ref[...])
        num = 0.0
        den = 0.0
        for m_ref, l_ref, acc_ref in chains:
            e = jnp.exp(m_ref[...] - mx)
            num = num + acc_ref[...] * e
            den = den + l_ref[...] * e
        o = num / den
        for hd in range(heads):
            o_ref[hd] = o[hd * seq:(hd + 1) * seq, :].astype(o_ref.dtype)

    @pl.when(t == last)
    def _():
        for i in range(pp):
            lat_copy(bn, gn, i, nslot).wait()
            krt_copy(bn, gn, i, nslot).wait()


def mla_attn_sample(page_table, ql, qr, c_new, kr_new, cache_lat, cache_krt, o_lat, layer,
                    *, row0, nb, seq, pp=32, nchains=1):
    heads, _, kv_lora = ql.shape
    rope = qr.shape[2]
    n_pages = page_table.shape[1]
    page = cache_lat.shape[2]
    pp = min(pp, n_pages)
    assert seq & (seq - 1) == 0 and pp % nchains == 0 and n_pages % pp == 0
    r = seq * heads
    blk0 = row0 // seq
    kern = functools.partial(_mla_attn_sample_kernel, layer=layer, pp=pp, page=page, seq=seq,
                             nchains=nchains)
    chain_stats = [pltpu.VMEM((r, 1), F32), pltpu.VMEM((r, 1), F32), pltpu.VMEM((r, kv_lora), F32)]
    grid_spec = pltpu.PrefetchScalarGridSpec(
        num_scalar_prefetch=1,
        grid=(nb, n_pages // pp),
        in_specs=[
            pl.BlockSpec((heads, seq, kv_lora), lambda b, g, pt: (0, blk0 + b, 0)),
            pl.BlockSpec((heads, seq, rope), lambda b, g, pt: (0, blk0 + b, 0)),
            pl.BlockSpec((seq, kv_lora), lambda b, g, pt: (blk0 + b, 0)),
            pl.BlockSpec((seq, rope), lambda b, g, pt: (blk0 + b, 0)),
            pl.BlockSpec(memory_space=pl.ANY),
            pl.BlockSpec(memory_space=pl.ANY),
            pl.BlockSpec(memory_space=pl.ANY),
        ],
        out_specs=pl.BlockSpec((heads, seq, kv_lora), lambda b, g, pt: (0, blk0 + b, 0)),
        scratch_shapes=[
            *(chain_stats * nchains),
            pltpu.VMEM((2, pp, page, kv_lora), F32), pltpu.VMEM((2, pp, rope, page), F32),
            pltpu.VMEM((pp * page, kv_lora), BF16), pltpu.VMEM((rope, pp * page), BF16),
            pltpu.VMEM((r, kv_lora), BF16), pltpu.VMEM((r, rope), BF16),
            pltpu.SemaphoreType.DMA((2,)), pltpu.SemaphoreType.DMA((2,)),
        ],
    )
    return pl.pallas_call(
        kern,
        grid_spec=grid_spec,
        out_shape=jax.ShapeDtypeStruct(o_lat.shape, o_lat.dtype),
        input_output_aliases={7: 0},
        compiler_params=_cparams("arbitrary", "arbitrary"),
        name="mla_attn_sample",
    )(page_table, ql, qr, c_new, kr_new, cache_lat, cache_krt, o_lat)


def _mla_post_kernel(ol_ref, wuv_ref, wo_ref, x_ref, o_ref, ob_ref, *, vdim):
    for hd in range(ol_ref.shape[0]):
        ob_ref[:, hd * vdim:(hd + 1) * vdim] = _dot(ol_ref[hd], wuv_ref[hd]).astype(BF16)
    o_ref[...] = x_ref[...] + _dot(ob_ref[...], wo_ref[...])


def mla_post(o_lat, wuv, wo, x, *, tm=256):
    m, d = x.shape
    heads, _, kv_lora = o_lat.shape
    vdim = wuv.shape[2]
    tm = min(tm, m)
    kern = functools.partial(_mla_post_kernel, vdim=vdim)
    return pl.pallas_call(
        kern,
        grid=(m // tm,),
        in_specs=[
            pl.BlockSpec((heads, tm, kv_lora), lambda i: (0, i, 0)),
            pl.BlockSpec(wuv.shape, lambda i: (0, 0, 0)),
            pl.BlockSpec(wo.shape, lambda i: (0, 0)),
            pl.BlockSpec((tm, d), lambda i: (i, 0)),
        ],
        out_specs=pl.BlockSpec((tm, d), lambda i: (i, 0)),
        out_shape=jax.ShapeDtypeStruct((m, d), F32),
        scratch_shapes=[pltpu.VMEM((tm, heads * vdim), BF16)],
        compiler_params=_cparams("parallel"),
        name="mla_post",
    )(o_lat, wuv, wo, x)


def _rot_half_cols(w, half):
    return jnp.concatenate([-w[..., half:], w[..., :half]], axis=-1)


def gla_layer(x, ln_g, w_in_t, w_gate_up, b_gate, g_norm, w_out, state_gla, states, layer, dims):
    nbp, lp, nbs, ls, heads, dk, dv, rank = dims
    nq, nv = heads * dk, heads * dv
    mp = nbp * lp
    g = ln_g.reshape(1, -1)
    lane = 128
    wa = jnp.pad(w_in_t[layer, 2 * nq + 2 * nv:], ((0, lane - rank), (0, 0))).astype(BF16)
    wg = jnp.pad(w_gate_up, ((0, lane - rank), (0, 0))).astype(BF16)
    qk = norm_matmul_t(x, g, w_in_t, layer, 0, 2 * nq, F32)
    vr = norm_matmul_t(x, g, w_in_t, layer, 2 * nq, 2 * nv, BF16)
    la = gla_gate(x, g, wa, wg, b_gate.reshape(1, -1))
    gn = g_norm.reshape(1, -1)
    sp_all, ss_all = states
    og, sp_all = gla_prompt(qk, vr, la, gn, sp_all, layer, nb=nbp, seq=lp, heads=heads, dk=dk, dv=dv,
                            n_layers=state_gla.shape[0])
    og, ss_all = gla_sample(qk, vr, la, gn, state_gla, og, ss_all, layer, row0=mp, nb=nbs, seq=ls,
                            heads=heads, dk=dk, dv=dv)
    x = matmul_res(og, w_out.astype(BF16), x)
    return x, (sp_all, ss_all)


def mla_layer(x, ln_g, w_in_t, g_q, w_qb, g_kv, w_uk, w_uv, w_out, cache_lat, cache_krt, page_table,
              cos, sin, layer, dims):
    nbp, lp, nbs, ls, heads, q_lora, kv_lora, nope, rope = dims
    mp = nbp * lp
    half = rope // 2
    o = q_lora + kv_lora
    w_kr = w_in_t[o:o + rope]
    win_ext = jnp.concatenate([w_in_t, -w_kr[half:], w_kr[:half]], axis=0).astype(BF16)
    wqb = w_qb.reshape(q_lora, heads, nope + rope)
    w_rope = wqb[:, :, nope:]
    wqb_ext = jnp.concatenate([
        wqb[:, :, :nope].reshape(q_lora, heads * nope),
        w_rope.reshape(q_lora, heads * rope),
        _rot_half_cols(w_rope, half).reshape(q_lora, heads * rope)], axis=1).astype(BF16)
    wuk_t = jnp.transpose(w_uk, (1, 2, 0)).astype(BF16)
    wuv = jnp.transpose(w_uv, (1, 0, 2)).astype(BF16)
    c_new, kr_new, cb, ct, krb, ql, qr = mla_pre(
        x, ln_g.reshape(1, -1), win_ext, g_q.reshape(1, -1), g_kv.reshape(1, -1), wqb_ext, wuk_t,
        cos, sin, heads=heads, q_lora=q_lora, kv_lora=kv_lora, nope=nope, rope=rope)
    o_lat = mla_attn_prompt(ql, qr, cb, krb, ct, nb=nbp, seq=lp)
    o_lat = mla_attn_sample(page_table, ql, qr, c_new, kr_new, cache_lat, cache_krt, o_lat, layer,
                            row0=mp, nb=nbs, seq=ls)
    x = mla_post(o_lat, wuv, w_out.astype(BF16), x)
    return x, c_new, kr_new


def kernel(x_prompt, x_sample, state_gla, cache_mla_latent, cache_mla_krope, page_table, ln_mixer_g, ln_ffn_g, gla_w_in, gla_w_gate_up, gla_b_gate, gla_norm_g, gla_w_out, mla_w_in, mla_q_norm_g, mla_w_qb, mla_kv_norm_g, mla_w_uk, mla_w_uv, mla_w_out, ffn_w_gate, ffn_w_up, ffn_w_down, final_norm_g):
    nbp, lp, d = x_prompt.shape
    nbs, ls, _ = x_sample.shape
    mp, ms = nbp * lp, nbs * ls
    depth = ln_mixer_g.shape[0]
    heads_g, dk, dv = state_gla.shape[2:]
    rank = gla_w_gate_up.shape[1]
    kv_lora, heads_m, nope = mla_w_uk.shape[1:]
    q_lora = mla_q_norm_g.shape[1]
    rope = cache_mla_krope.shape[3]
    past_len = page_table.shape[1] * cache_mla_latent.shape[2]

    x = jnp.concatenate([x_prompt.reshape(mp, d), x_sample.reshape(ms, d)], axis=0)

    half = rope // 2
    inv = 1.0 / (ROPE_THETA ** (jnp.arange(half, dtype=F32) / half))
    pos = jnp.concatenate([jnp.tile(jnp.arange(lp, dtype=jnp.int32), nbp),
                           jnp.tile(past_len + jnp.arange(ls, dtype=jnp.int32), nbs)])
    ang = pos.astype(F32)[:, None] * inv[None, :]
    cos = jnp.tile(jnp.cos(ang), (1, 2 * rope // half))
    sin = jnp.tile(jnp.sin(ang), (1, 2 * rope // half))

    gla_dims = (nbp, lp, nbs, ls, heads_g, dk, dv, rank)
    mla_dims = (nbp, lp, nbs, ls, heads_m, q_lora, kv_lora, nope, rope)
    cache_krt = jnp.swapaxes(cache_mla_krope, 2, 3)
    gla_w_in_t = jnp.swapaxes(gla_w_in, 1, 2)
    mla_w_in_t = jnp.swapaxes(mla_w_in, 1, 2)
    gla_states = (None, None)
    mla_c, mla_kr = [], []
    for i in range(depth):
        j = i // 2
        if i % 2 == 0:
            x, gla_states = gla_layer(x, ln_mixer_g[i], gla_w_in_t, gla_w_gate_up[j], gla_b_gate[j],
                                      gla_norm_g[j], gla_w_out[j], state_gla, gla_states, j, gla_dims)
        else:
            x, c_new, kr_new = mla_layer(x, ln_mixer_g[i], mla_w_in_t[j], mla_q_norm_g[j], mla_w_qb[j],
                                         mla_kv_norm_g[j], mla_w_uk[j], mla_w_uv[j], mla_w_out[j],
                                         cache_mla_latent, cache_krt, page_table, cos, sin, j,
                                         mla_dims)
            mla_c.append(c_new)
            mla_kr.append(kr_new)
        x = ffn(x, ln_ffn_g[i].reshape(1, -1), ffn_w_gate, ffn_w_up, ffn_w_down, i)

    fg = final_norm_g.reshape(1, -1)
    y_prompt = rmsnorm_rows(x, fg, 0, mp).reshape(nbp, lp, d)
    y_sample = rmsnorm_rows(x, fg, mp, ms).reshape(nbs, ls, d)
    c_all = jnp.stack(mla_c)
    kr_all = jnp.stack(mla_kr)
    return (y_prompt, y_sample, gla_states[0], gla_states[1],
            c_all[:, :mp].reshape(-1, nbp, lp, kv_lora), kr_all[:, :mp].reshape(-1, nbp, lp, rope),
            c_all[:, mp:].reshape(-1, nbs, ls, kv_lora), kr_all[:, mp:].reshape(-1, nbs, ls, rope))
```

```python
import functools

import jax
import jax.numpy as jnp
from jax import lax
from jax.experimental import pallas as pl
from jax.experimental.pallas import tpu as pltpu

F32 = jnp.float32
BF16 = jnp.bfloat16

RMS_EPS = 1e-6
ROPE_THETA = 10000.0
GLA_GATE_TEMP = 16.0
GLA_CHUNK = 64
GLA_SUB = 16
EXP_CLAMP = 80.0
NEG_BIG = -1e30

V7X_VMEM_BYTES = 64 * 1024 * 1024
VMEM_LIMIT_BYTES = V7X_VMEM_BYTES - 8 * 1024 * 1024


def _cparams(*sem):
    return pltpu.CompilerParams(dimension_semantics=sem, vmem_limit_bytes=VMEM_LIMIT_BYTES)


def _rms(x, g):
    ms = jnp.mean(x * x, axis=-1, keepdims=True)
    return x * lax.rsqrt(ms + RMS_EPS) * g


def _silu(x):
    return x * jax.nn.sigmoid(x)


def _dot(a, b):
    return jnp.dot(a, b, preferred_element_type=F32)


def _dot_nt(a, b):
    return lax.dot_general(a, b, (((1,), (1,)), ((), ())), preferred_element_type=F32)


def _dot_tn(a, b):
    return lax.dot_general(a, b, (((0,), (0,)), ((), ())), preferred_element_type=F32)


def _norm_matmul_kernel(x_ref, g_ref, w_ref, o_ref, h_ref):
    @pl.when(pl.program_id(1) == 0)
    def _():
        h_ref[...] = _rms(x_ref[...], g_ref[...]).astype(BF16)

    o_ref[...] = _dot(h_ref[...], w_ref[...]).astype(o_ref.dtype)


def norm_matmul(x, g, w, out_dtype, *, tm=512, tn=1024):
    m, d = x.shape
    n = w.shape[1]
    tm, tn = min(tm, m), min(tn, n)
    return pl.pallas_call(
        _norm_matmul_kernel,
        grid=(m // tm, n // tn),
        in_specs=[
            pl.BlockSpec((tm, d), lambda i, j: (i, 0)),
            pl.BlockSpec((1, d), lambda i, j: (0, 0)),
            pl.BlockSpec((d, tn), lambda i, j: (0, j)),
        ],
        out_specs=pl.BlockSpec((tm, tn), lambda i, j: (i, j)),
        out_shape=jax.ShapeDtypeStruct((m, n), out_dtype),
        scratch_shapes=[pltpu.VMEM((tm, d), BF16)],
        compiler_params=_cparams("parallel", "arbitrary"),
        name="norm_matmul",
    )(x, g, w)


def _norm_matmul_t_kernel(x_ref, g_ref, wt_ref, o_ref, wb_ref):
    @pl.when(pl.program_id(1) == 0)
    def _():
        wb_ref[...] = wt_ref[...].astype(BF16)

    h = _rms(x_ref[...], g_ref[...]).astype(BF16)
    o_ref[...] = _dot_nt(h, wb_ref[...]).astype(o_ref.dtype)


def norm_matmul_t(x, g, wt, layer, row0, n, out_dtype, *, tm=512, tn=1024):
    m, d = x.shape
    tm, tn = min(tm, m), min(tn, n)
    assert row0 % tn == 0 and n % tn == 0 and m % tm == 0
    blk0 = row0 // tn
    return pl.pallas_call(
        _norm_matmul_t_kernel,
        grid=(n // tn, m // tm),
        in_specs=[
            pl.BlockSpec((tm, d), lambda j, i: (i, 0)),
            pl.BlockSpec((1, d), lambda j, i: (0, 0)),
            pl.BlockSpec((None, tn, d), lambda j, i: (layer, blk0 + j, 0)),
        ],
        out_specs=pl.BlockSpec((tm, tn), lambda j, i: (i, j)),
        out_shape=jax.ShapeDtypeStruct((m, n), out_dtype),
        scratch_shapes=[pltpu.VMEM((tn, d), BF16)],
        compiler_params=_cparams("parallel", "arbitrary"),
        name="norm_matmul_t",
    )(x, g, wt)


def _matmul_res_kernel(a_ref, w_ref, r_ref, o_ref):
    o_ref[...] = r_ref[...] + _dot(a_ref[...], w_ref[...])


def matmul_res(a, w, res, *, tm=512, tn=1024):
    m, k = a.shape
    n = w.shape[1]
    tm, tn = min(tm, m), min(tn, n)
    return pl.pallas_call(
        _matmul_res_kernel,
        grid=(m // tm, n // tn),
        in_specs=[
            pl.BlockSpec((tm, k), lambda i, j: (i, 0)),
            pl.BlockSpec((k, tn), lambda i, j: (0, j)),
            pl.BlockSpec((tm, tn), lambda i, j: (i, j)),
        ],
        out_specs=pl.BlockSpec((tm, tn), lambda i, j: (i, j)),
        out_shape=jax.ShapeDtypeStruct((m, n), F32),
        compiler_params=_cparams("parallel", "arbitrary"),
        name="matmul_res",
    )(a, w, res)


def _ffn_kernel(x_ref, g_ref, wg_ref, wu_ref, wd_ref, o_ref, h_ref):
    @pl.when(pl.program_id(1) == 0)
    def _():
        x = x_ref[...]
        h_ref[...] = _rms(x, g_ref[...]).astype(BF16)
        o_ref[...] = x

    h = h_ref[...]
    gate = _dot(h, wg_ref[...].astype(BF16))
    up = _dot(h, wu_ref[...].astype(BF16))
    a = (_silu(gate) * up).astype(BF16)
    o_ref[...] += _dot(a, wd_ref[...].astype(BF16))


def ffn(x, g, wg, wu, wd, layer, *, tm=1024, tf=256):
    m, d = x.shape
    f = wg.shape[2]
    tm, tf = min(tm, m), min(tf, f)
    while m % tm:
        tm //= 2
    assert f % tf == 0
    return pl.pallas_call(
        _ffn_kernel,
        grid=(m // tm, f // tf),
        in_specs=[
            pl.BlockSpec((tm, d), lambda i, j: (i, 0)),
            pl.BlockSpec((1, d), lambda i, j: (0, 0)),
            pl.BlockSpec((None, d, tf), lambda i, j: (layer, 0, j)),
            pl.BlockSpec((None, d, tf), lambda i, j: (layer, 0, j)),
            pl.BlockSpec((None, tf, d), lambda i, j: (layer, j, 0)),
        ],
        out_specs=pl.BlockSpec((tm, d), lambda i, j: (i, 0)),
        out_shape=jax.ShapeDtypeStruct((m, d), F32),
        scratch_shapes=[pltpu.VMEM((tm, d), BF16)],
        compiler_params=_cparams("parallel", "arbitrary"),
        name="ffn",
    )(x, g, wg, wu, wd)


def _rmsnorm_kernel(x_ref, g_ref, o_ref):
    o_ref[...] = _rms(x_ref[...], g_ref[...])


def rmsnorm_rows(x, g, row0, nrows, *, tm=512):
    d = x.shape[1]
    tm = min(tm, nrows)
    blk0 = row0 // tm
    return pl.pallas_call(
        _rmsnorm_kernel,
        grid=(nrows // tm,),
        in_specs=[
            pl.BlockSpec((tm, d), lambda i: (blk0 + i, 0)),
            pl.BlockSpec((1, d), lambda i: (0, 0)),
        ],
        out_specs=pl.BlockSpec((tm, d), lambda i: (i, 0)),
        out_shape=jax.ShapeDtypeStruct((nrows, d), F32),
        compiler_params=_cparams("parallel"),
        name="final_rmsnorm",
    )(x, g)


def _gla_gate_kernel(x_ref, g_ref, wa_ref, wg_ref, bg_ref, o_ref):
    h = _rms(x_ref[...], g_ref[...]).astype(BF16)
    a = _dot_nt(h, wa_ref[...]).astype(BF16)
    z = _dot(a, wg_ref[...]) + bg_ref[...]
    log_sig = jnp.minimum(z, 0.0) - jnp.log1p(jnp.exp(-jnp.abs(z)))
    o_ref[...] = log_sig * (1.0 / GLA_GATE_TEMP)


def gla_gate(x, g, wa, wg, bg, *, tm=512):
    m, d = x.shape
    n = wg.shape[1]
    tm = min(tm, m)
    return pl.pallas_call(
        _gla_gate_kernel,
        grid=(m // tm,),
        in_specs=[
            pl.BlockSpec((tm, d), lambda i: (i, 0)),
            pl.BlockSpec((1, d), lambda i: (0, 0)),
            pl.BlockSpec(wa.shape, lambda i: (0, 0)),
            pl.BlockSpec(wg.shape, lambda i: (0, 0)),
            pl.BlockSpec((1, n), lambda i: (0, 0)),
        ],
        out_specs=pl.BlockSpec((tm, n), lambda i: (i, 0)),
        out_shape=jax.ShapeDtypeStruct((m, n), F32),
        compiler_params=_cparams("parallel"),
        name="gla_gate",
    )(x, g, wa, wg, bg)


def _cumsum_rows(la):
    c = la.shape[0]
    if c <= 8:
        rows = lax.broadcasted_iota(jnp.int32, la.shape, 0)
        out = jnp.zeros_like(la)
        for j in range(c):
            out = out + jnp.where(rows >= j, la[j:j + 1, :], 0.0)
        return out
    r = lax.broadcasted_iota(jnp.int32, (c, c), 0)
    s = lax.broadcasted_iota(jnp.int32, (c, c), 1)
    tri = jnp.where(s <= r, 1.0, 0.0).astype(BF16)
    hi = la.astype(BF16)
    lo = (la - hi.astype(F32)).astype(BF16)
    return _dot(tri, hi) + _dot(tri, lo)


def _gla_chunk(qs, k, v, la, s, sub):
    c, dk = qs.shape
    b = _cumsum_rows(la)
    b_last = b[c - 1:c, :]
    o = _dot((qs * jnp.exp(b)).astype(BF16), s.astype(BF16))
    col = lax.broadcasted_iota(jnp.int32, (sub, c), 1)
    row = lax.broadcasted_iota(jnp.int32, (sub, c), 0)
    parts = []
    for i in range(c // sub):
        lo = i * sub
        ref = b[lo - 1:lo, :] if i else jnp.zeros((1, dk), F32)
        qi = (qs[lo:lo + sub, :] * jnp.exp(b[lo:lo + sub, :] - ref)).astype(BF16)
        kd = (k * jnp.exp(jnp.minimum(ref - b, EXP_CLAMP))).astype(BF16)
        parts.append(jnp.where(col <= row + lo, _dot_nt(qi, kd), 0.0))
    att = parts[0] if len(parts) == 1 else jnp.concatenate(parts, axis=0)
    o = o + _dot(att.astype(BF16), v)
    kdl = (k * jnp.exp(b_last - b)).astype(BF16)
    decay_col = jnp.exp(b[c - 8:c, :].T[:, 7:8])
    s_new = decay_col * s + _dot_tn(kdl, v)
    return o, s_new


def _gla_gate_out(o, r, gn):
    return (_rms(o, gn) * _silu(r.astype(F32))).astype(BF16)


def _gla_prompt_kernel(q_ref, k_ref, v_ref, r_ref, la_ref, gn_ref, o_ref, s_ref,
                       *, chunk, sub, scale, unroll):
    @pl.when(pl.program_id(2) == 0)
    def _():
        s_ref[...] = jnp.zeros_like(s_ref)

    hg, dk, dv = s_ref.shape

    def body(c, carry):
        rows = pl.ds(pl.multiple_of(c * chunk, chunk), chunk)
        for j in range(hg):
            kc = slice(j * dk, (j + 1) * dk)
            vc = slice(j * dv, (j + 1) * dv)
            o, s_new = _gla_chunk(q_ref[rows, kc] * scale, k_ref[rows, kc], v_ref[rows, vc],
                                  la_ref[rows, kc], s_ref[j], sub)
            s_ref[j] = s_new
            o_ref[rows, vc] = _gla_gate_out(o, r_ref[rows, vc], gn_ref[...])
        return carry

    lax.fori_loop(0, q_ref.shape[0] // chunk, body, 0, unroll=unroll)


def gla_prompt(qk, vr, la, gn, s_all, layer, *, nb, seq, heads, dk, dv, n_layers, tb=512, hg=4,
               unroll=1):
    m = qk.shape[0]
    chunk = min(GLA_CHUNK, seq)
    tb = min(tb, seq)
    nt = seq // tb
    ngrp = heads // hg
    assert ngrp * hg == heads
    row = lambda b, h, t: b * nt + t
    in_specs = [
        pl.BlockSpec((tb, hg * dk), lambda b, h, t: (row(b, h, t), h)),
        pl.BlockSpec((tb, hg * dk), lambda b, h, t: (row(b, h, t), ngrp + h)),
        pl.BlockSpec((tb, hg * dv), lambda b, h, t: (row(b, h, t), h)),
        pl.BlockSpec((tb, hg * dv), lambda b, h, t: (row(b, h, t), ngrp + h)),
        pl.BlockSpec((tb, hg * dk), lambda b, h, t: (row(b, h, t), h)),
        pl.BlockSpec((1, dv), lambda b, h, t: (0, 0)),
    ]
    args = [qk, qk, vr, vr, la, gn]
    aliases = {}
    if s_all is not None:
        in_specs.append(pl.BlockSpec(memory_space=pl.ANY))
        args.append(s_all)
        aliases = {len(args) - 1: 1}
    kern = functools.partial(_gla_prompt_kernel, chunk=chunk, sub=min(GLA_SUB, chunk), scale=dk ** -0.5,
                             unroll=unroll)
    if s_all is not None:
        kern = functools.partial(_drop_arg, kern, 6)
    return pl.pallas_call(
        kern,
        grid=(nb, ngrp, nt),
        in_specs=in_specs,
        out_specs=[
            pl.BlockSpec((tb, hg * dv), lambda b, h, t: (row(b, h, t), h)),
            pl.BlockSpec((None, None, hg, dk, dv), lambda b, h, t: (layer, b, h, 0, 0)),
        ],
        out_shape=[
            jax.ShapeDtypeStruct((m, heads * dv), BF16),
            jax.ShapeDtypeStruct((n_layers, nb, heads, dk, dv), F32),
        ],
        input_output_aliases=aliases,
        compiler_params=_cparams("parallel", "parallel", "arbitrary"),
        name="gla_prompt",
    )(*args)


def _drop_arg(fn, idx, *refs):
    return fn(*refs[:idx], *refs[idx + 1:])


def _drop_args(fn, idxs, *refs):
    return fn(*[r for i, r in enumerate(refs) if i not in idxs])


def _gla_sample_kernel(q_ref, k_ref, v_ref, r_ref, la_ref, gn_ref, s0_ref, o_ref, s_ref,
                       *, seq, gb, heads, dk, dv, scale):
    for g in range(gb):
        rows = slice(g * seq, (g + 1) * seq)
        for h in range(heads):
            kc = slice(h * dk, (h + 1) * dk)
            vc = slice(h * dv, (h + 1) * dv)
            o, s_new = _gla_chunk(q_ref[rows, kc] * scale, k_ref[rows, kc], v_ref[rows, vc],
                                  la_ref[rows, kc], s0_ref[g, h], seq)
            s_ref[g, h] = s_new
            o_ref[rows, vc] = _gla_gate_out(o, r_ref[rows, vc], gn_ref[...])


def gla_sample(qk, vr, la, gn, state, og, s_all, layer, *, row0, nb, seq, heads, dk, dv, gb=2):
    m = qk.shape[0]
    n_layers = state.shape[0]
    tb = gb * seq
    blk0 = row0 // tb
    in_specs = [
        pl.BlockSpec((tb, heads * dk), lambda i: (blk0 + i, 0)),
        pl.BlockSpec((tb, heads * dk), lambda i: (blk0 + i, 1)),
        pl.BlockSpec((tb, heads * dv), lambda i: (blk0 + i, 0)),
        pl.BlockSpec((tb, heads * dv), lambda i: (blk0 + i, 1)),
        pl.BlockSpec((tb, heads * dk), lambda i: (blk0 + i, 0)),
        pl.BlockSpec((1, dv), lambda i: (0, 0)),
        pl.BlockSpec((None, gb, heads, dk, dv), lambda i: (layer, i, 0, 0, 0)),
        pl.BlockSpec(memory_space=pl.ANY),
    ]
    args = [qk, qk, vr, vr, la, gn, state, og]
    aliases = {7: 0}
    drop = [7]
    if s_all is not None:
        in_specs.append(pl.BlockSpec(memory_space=pl.ANY))
        args.append(s_all)
        aliases[8] = 1
        drop.append(8)
    kern = functools.partial(_gla_sample_kernel, seq=seq, gb=gb, heads=heads, dk=dk, dv=dv,
                             scale=dk ** -0.5)
    kern = functools.partial(_drop_args, kern, tuple(drop))
    return pl.pallas_call(
        kern,
        grid=(nb // gb,),
        in_specs=in_specs,
        out_specs=[
            pl.BlockSpec((tb, heads * dv), lambda i: (blk0 + i, 0)),
            pl.BlockSpec((None, gb, heads, dk, dv), lambda i: (layer, i, 0, 0, 0)),
        ],
        out_shape=[
            jax.ShapeDtypeStruct((m, heads * dv), BF16),
            jax.ShapeDtypeStruct((n_layers, nb, heads, dk, dv), F32),
        ],
        input_output_aliases=aliases,
        compiler_params=_cparams("parallel"),
        name="gla_sample",
    )(*args)


def _mla_pre_kernel(x_ref, g_ref, win_ref, gq_ref, gkv_ref, wqb_ref, wuk_ref, cos_ref, sin_ref,
                    c_ref, kr_ref, cb_ref, ct_ref, krb_ref, ql_ref, qr_ref,
                    *, heads, q_lora, kv_lora, nope, rope, scale):
    h = _rms(x_ref[...], g_ref[...]).astype(BF16)
    p = _dot_nt(h, win_ref[...])
    cos = cos_ref[...]
    sin = sin_ref[...]
    c = _rms(p[:, q_lora:q_lora + kv_lora], gkv_ref[...])
    c_ref[...] = c
    cb_ref[...] = c.astype(BF16)
    ct_ref[...] = c.T.astype(BF16)
    o = q_lora + kv_lora
    kr = p[:, o:o + rope] * cos[:, :rope] + p[:, o + rope:o + 2 * rope] * sin[:, :rope]
    kr_ref[...] = kr
    krb_ref[...] = kr.astype(BF16)
    cqn = _rms(p[:, :q_lora], gq_ref[...]).astype(BF16)
    q = _dot(cqn, wqb_ref[...])
    n0 = heads * nope
    n1 = n0 + heads * rope
    pair = 2 * rope
    for hp in range(heads * rope // pair):
        a = q[:, n0 + hp * pair:n0 + (hp + 1) * pair]
        b = q[:, n1 + hp * pair:n1 + (hp + 1) * pair]
        two = ((a * cos + b * sin) * scale).astype(BF16)
        qr_ref[2 * hp] = two[:, :rope]
        qr_ref[2 * hp + 1] = two[:, rope:]
    for hd in range(heads):
        qn = q[:, hd * nope:(hd + 1) * nope].astype(BF16)
        ql_ref[hd] = (_dot(qn, wuk_ref[hd]) * scale).astype(BF16)


def mla_pre(x, g, win_ext, gq, gkv, wqb_ext, wuk_t, cos, sin, *, heads, q_lora, kv_lora, nope, rope,
            tm=256):
    m, d = x.shape
    tm = min(tm, m)
    full = lambda a: pl.BlockSpec(a.shape, lambda i: (0,) * a.ndim)
    rows = lambda n: pl.BlockSpec((tm, n), lambda i: (i, 0))
    hrows = lambda n: pl.BlockSpec((heads, tm, n), lambda i: (0, i, 0))
    kern = functools.partial(_mla_pre_kernel, heads=heads, q_lora=q_lora, kv_lora=kv_lora, nope=nope,
                             rope=rope, scale=(nope + rope) ** -0.5)
    return pl.pallas_call(
        kern,
        grid=(m // tm,),
        in_specs=[rows(d), full(g), full(win_ext), full(gq), full(gkv), full(wqb_ext), full(wuk_t),
                  rows(cos.shape[1]), rows(sin.shape[1])],
        out_specs=[rows(kv_lora), rows(rope), rows(kv_lora),
                   pl.BlockSpec((kv_lora, tm), lambda i: (0, i)), rows(rope),
                   hrows(kv_lora), hrows(rope)],
        out_shape=[
            jax.ShapeDtypeStruct((m, kv_lora), F32),
            jax.ShapeDtypeStruct((m, rope), F32),
            jax.ShapeDtypeStruct((m, kv_lora), BF16),
            jax.ShapeDtypeStruct((kv_lora, m), BF16),
            jax.ShapeDtypeStruct((m, rope), BF16),
            jax.ShapeDtypeStruct((heads, m, kv_lora), BF16),
            jax.ShapeDtypeStruct((heads, m, rope), BF16),
        ],
        compiler_params=_cparams("parallel"),
        name="mla_pre",
    )(x, g, win_ext, gq, gkv, wqb_ext, wuk_t, cos, sin)


def _flash_update(s, v, m_ref, l_ref, acc_ref):
    m_prev = m_ref[...]
    m_new = jnp.maximum(m_prev, jnp.max(s, axis=-1, keepdims=True))
    alpha = jnp.exp(m_prev - m_new)
    p = jnp.exp(s - m_new)
    l_ref[...] = alpha * l_ref[...] + jnp.sum(p, axis=-1, keepdims=True)
    acc_ref[...] = alpha * acc_ref[...] + _dot(p.astype(BF16), v)
    m_ref[...] = m_new


def _flash_update_t(st, vt, m_ref, l_ref, acc_ref):
    m_prev = m_ref[...]
    m_new = jnp.maximum(m_prev, jnp.max(st, axis=0, keepdims=True))
    alpha = jnp.exp(m_prev - m_new)
    p = jnp.exp(st - m_new)
    l_ref[...] = alpha * l_ref[...] + jnp.sum(p, axis=0, keepdims=True)
    acc_ref[...] = alpha * acc_ref[...] + _dot(vt, p.astype(BF16))
    m_ref[...] = m_new


def _flash_init(m_ref, l_ref, acc_ref):
    m_ref[...] = jnp.full_like(m_ref, NEG_BIG)
    l_ref[...] = jnp.zeros_like(l_ref)
    acc_ref[...] = jnp.zeros_like(acc_ref)


def _mla_attn_prompt_kernel(qi_ref, ki_ref, q_ref, qr_ref, c_ref, kr_ref, ct_ref, o_ref, *stats,
                            qb, tk, nsplit):
    step = pl.program_id(1)
    qi = qi_ref[step]
    ki = ki_ref[step]
    heads = q_ref.shape[0]
    gb = qb // nsplit
    r = heads * gb
    groups = [stats[3 * g:3 * g + 3] for g in range(nsplit)]

    @pl.when(ki == 0)
    def _():
        for m_ref, l_ref, acc_ref in groups:
            _flash_init(m_ref, l_ref, acc_ref)

    def update(masked):
        for g, (m_ref, l_ref, acc_ref) in enumerate(groups):
            q = q_ref[:, g * gb:(g + 1) * gb, :].reshape(r, q_ref.shape[2])
            qr = qr_ref[:, g * gb:(g + 1) * gb, :].reshape(r, qr_ref.shape[2])
            st = _dot_nt(c_ref[...], q) + _dot_nt(kr_ref[...], qr)
            if masked:
                kpos = ki * tk + lax.broadcasted_iota(jnp.int32, st.shape, 0)
                qpos = qi * qb + g * gb + (lax.broadcasted_iota(jnp.int32, st.shape, 1) & (gb - 1))
                st = jnp.where(kpos <= qpos, st, NEG_BIG)
            _flash_update_t(st, ct_ref[...], m_ref, l_ref, acc_ref)

    crosses_diagonal = ki * tk + tk - 1 > qi * qb
    pl.when(crosses_diagonal)(functools.partial(update, True))
    pl.when(jnp.logical_not(crosses_diagonal))(functools.partial(update, False))

    @pl.when(ki == (qi * qb + qb - 1) // tk)
    def _():
        for g, (m_ref, l_ref, acc_ref) in enumerate(groups):
            o = (acc_ref[...] * (1.0 / l_ref[...])).T
            for hd in range(heads):
                o_ref[hd, g * gb:(g + 1) * gb, :] = o[hd * gb:(hd + 1) * gb, :].astype(o_ref.dtype)


def mla_attn_prompt(ql, qr, cb, krb, ct, *, nb, seq, qb=128, tk=512, nsplit=2):
    heads, m, kv_lora = ql.shape
    rope = qr.shape[2]
    qb, tk = min(qb, seq), min(tk, seq)
    gb = qb // nsplit
    assert gb & (gb - 1) == 0 and gb * nsplit == qb
    r = gb * heads
    nq, nk = seq // qb, seq // tk
    pairs = [(qi, ki) for qi in range(nq) for ki in range((qi * qb + qb - 1) // tk + 1)]
    qi_tab = jnp.asarray([p[0] for p in pairs], jnp.int32)
    ki_tab = jnp.asarray([p[1] for p in pairs], jnp.int32)
    qmap = lambda b, s, qt, kt: (0, b * nq + qt[s], 0)
    kern = functools.partial(_mla_attn_prompt_kernel, qb=qb, tk=tk, nsplit=nsplit)
    group_stats = [pltpu.VMEM((1, r), F32), pltpu.VMEM((1, r), F32), pltpu.VMEM((kv_lora, r), F32)]
    grid_spec = pltpu.PrefetchScalarGridSpec(
        num_scalar_prefetch=2,
        grid=(nb, len(pairs)),
        in_specs=[
            pl.BlockSpec((heads, qb, kv_lora), qmap),
            pl.BlockSpec((heads, qb, rope), qmap),
            pl.BlockSpec((tk, kv_lora), lambda b, s, qt, kt: (b * nk + kt[s], 0)),
            pl.BlockSpec((tk, rope), lambda b, s, qt, kt: (b * nk + kt[s], 0)),
            pl.BlockSpec((kv_lora, tk), lambda b, s, qt, kt: (0, b * nk + kt[s])),
        ],
        out_specs=pl.BlockSpec((heads, qb, kv_lora), qmap),
        scratch_shapes=group_stats * nsplit,
    )
    return pl.pallas_call(
        kern,
        grid_spec=grid_spec,
        out_shape=jax.ShapeDtypeStruct((heads, m, kv_lora), BF16),
        compiler_params=_cparams("parallel", "arbitrary"),
        name="mla_attn_prompt",
    )(qi_tab, ki_tab, ql, qr, cb, krb, ct)


def _mla_attn_sample_kernel(pt_ref, q_ref, qr_ref, cn_ref, krn_ref, lat_hbm, krt_hbm, o_alias,
                            o_ref, *scratch, layer, pp, page, seq, nchains):
    del o_alias
    chains = [scratch[3 * c:3 * c + 3] for c in range(nchains)]
    latbuf, krtbuf, kbuf_ref, krtb_ref, q2_ref, qr2_ref, lat_sem, krt_sem = scratch[3 * nchains:]
    b = pl.program_id(0)
    g = pl.program_id(1)
    ng = pl.num_programs(1)
    heads = q_ref.shape[0]
    t = b * ng + g
    last = pl.num_programs(0) * ng - 1
    slot = lax.rem(t, 2)
    nslot = 1 - slot
    tn = jnp.minimum(t + 1, last)
    bn = lax.div(tn, ng)
    gn = lax.rem(tn, ng)

    def lat_copy(bb, gg, i, s):
        pg = pt_ref[bb, gg * pp + i]
        return pltpu.make_async_copy(lat_hbm.at[layer, pg], latbuf.at[s, i], lat_sem.at[s])

    def krt_copy(bb, gg, i, s):
        pg = pt_ref[bb, gg * pp + i]
        return pltpu.make_async_copy(krt_hbm.at[layer, pg], krtbuf.at[s, i], krt_sem.at[s])

    @pl.when(t == 0)
    def _():
        for i in range(pp):
            lat_copy(b, g, i, 0).start()
            krt_copy(b, g, i, 0).start()

    @pl.when(g == 0)
    def _():
        for m_ref, l_ref, acc_ref in chains:
            _flash_init(m_ref, l_ref, acc_ref)
        q2_ref[...] = jnp.concatenate(
            [q_ref[hd].astype(F32) for hd in range(heads)], axis=0).astype(BF16)
        qr2_ref[...] = jnp.concatenate(
            [qr_ref[hd].astype(F32) for hd in range(heads)], axis=0).astype(BF16)

    for i in range(pp):
        lat_copy(bn, gn, i, nslot).start()
    for i in range(pp):
        lat_copy(b, g, i, slot).wait()
        krt_copy(b, g, i, slot).wait()
    for i in range(pp):
        krt_copy(bn, gn, i, nslot).start()
        kbuf_ref[i * page:(i + 1) * page, :] = latbuf[slot, i].astype(BF16)
        krtb_ref[:, i * page:(i + 1) * page] = krtbuf[slot, i].astype(BF16)
    q = q2_ref[...]
    qr = qr2_ref[...]
    part = pp * page // nchains
    for ch, (m_ref, l_ref, acc_ref) in enumerate(chains):
        kb = kbuf_ref[ch * part:(ch + 1) * part, :]
        s = _dot_nt(q, kb) + _dot(qr, krtb_ref[:, ch * part:(ch + 1) * part])
        _flash_update(s, kb, m_ref, l_ref, acc_ref)

    @pl.when(g == ng - 1)
    def _():
        cn = cn_ref[...].astype(BF16)
        s = _dot_nt(q, cn) + _dot_nt(qr, krn_ref[...].astype(BF16))
        qtok = lax.broadcasted_iota(jnp.int32, s.shape, 0) & (seq - 1)
        ktok = lax.broadcasted_iota(jnp.int32, s.shape, 1)
        _flash_update(jnp.where(ktok <= qtok, s, NEG_BIG), cn, *chains[0])
        mx = chains[0][0][...]
        for m_ref, _, _ in chains[1:]:
            mx = jnp.maximum(mx, m_ref[...])
        num = 0.0
        den = 0.0
        for m_ref, l_ref, acc_ref in chains:
            e = jnp.exp(m_ref[...] - mx)
            num = num + acc_ref[...] * e
            den = den + l_ref[...] * e
        o = num / den
        for hd in range(heads):
            o_ref[hd] = o[hd * seq:(hd + 1) * seq, :].astype(o_ref.dtype)

    @pl.when(t == last)
    def _():
        for i in range(pp):
            lat_copy(bn, gn, i, nslot).wait()
            krt_copy(bn, gn, i, nslot).wait()


def mla_attn_sample(page_table, ql, qr, c_new, kr_new, cache_lat, cache_krt, o_lat, layer,
                    *, row0, nb, seq, pp=32, nchains=1):
    heads, _, kv_lora = ql.shape
    rope = qr.shape[2]
    n_pages = page_table.shape[1]
    page = cache_lat.shape[2]
    pp = min(pp, n_pages)
    assert seq & (seq - 1) == 0 and pp % nchains == 0 and n_pages % pp == 0
    r = seq * heads
    blk0 = row0 // seq
    kern = functools.partial(_mla_attn_sample_kernel, layer=layer, pp=pp, page=page, seq=seq,
                             nchains=nchains)
    chain_stats = [pltpu.VMEM((r, 1), F32), pltpu.VMEM((r, 1), F32), pltpu.VMEM((r, kv_lora), F32)]
    grid_spec = pltpu.PrefetchScalarGridSpec(
        num_scalar_prefetch=1,
        grid=(nb, n_pages // pp),
        in_specs=[
            pl.BlockSpec((heads, seq, kv_lora), lambda b, g, pt: (0, blk0 + b, 0)),
            pl.BlockSpec((heads, seq, rope), lambda b, g, pt: (0, blk0 + b, 0)),
            pl.BlockSpec((seq, kv_lora), lambda b, g, pt: (blk0 + b, 0)),
            pl.BlockSpec((seq, rope), lambda b, g, pt: (blk0 + b, 0)),
            pl.BlockSpec(memory_space=pl.ANY),
            pl.BlockSpec(memory_space=pl.ANY),
            pl.BlockSpec(memory_space=pl.ANY),
        ],
        out_specs=pl.BlockSpec((heads, seq, kv_lora), lambda b, g, pt: (0, blk0 + b, 0)),
        scratch_shapes=[
            *(chain_stats * nchains),
            pltpu.VMEM((2, pp, page, kv_lora), F32), pltpu.VMEM((2, pp, rope, page), F32),
            pltpu.VMEM((pp * page, kv_lora), BF16), pltpu.VMEM((rope, pp * page), BF16),
            pltpu.VMEM((r, kv_lora), BF16), pltpu.VMEM((r, rope), BF16),
            pltpu.SemaphoreType.DMA((2,)), pltpu.SemaphoreType.DMA((2,)),
        ],
    )
    return pl.pallas_call(
        kern,
        grid_spec=grid_spec,
        out_shape=jax.ShapeDtypeStruct(o_lat.shape, o_lat.dtype),
        input_output_aliases={7: 0},
        compiler_params=_cparams("arbitrary", "arbitrary"),
        name="mla_attn_sample",
    )(page_table, ql, qr, c_new, kr_new, cache_lat, cache_krt, o_lat)


def _mla_post_kernel(ol_ref, wuv_ref, wo_ref, x_ref, o_ref, ob_ref, *, vdim):
    for hd in range(ol_ref.shape[0]):
        ob_ref[:, hd * vdim:(hd + 1) * vdim] = _dot(ol_ref[hd], wuv_ref[hd]).astype(BF16)
    o_ref[...] = x_ref[...] + _dot(ob_ref[...], wo_ref[...])


def mla_post(o_lat, wuv, wo, x, *, tm=256):
    m, d = x.shape
    heads, _, kv_lora = o_lat.shape
    vdim = wuv.shape[2]
    tm = min(tm, m)
    kern = functools.partial(_mla_post_kernel, vdim=vdim)
    return pl.pallas_call(
        kern,
        grid=(m // tm,),
        in_specs=[
            pl.BlockSpec((heads, tm, kv_lora), lambda i: (0, i, 0)),
            pl.BlockSpec(wuv.shape, lambda i: (0, 0, 0)),
            pl.BlockSpec(wo.shape, lambda i: (0, 0)),
            pl.BlockSpec((tm, d), lambda i: (i, 0)),
        ],
        out_specs=pl.BlockSpec((tm, d), lambda i: (i, 0)),
        out_shape=jax.ShapeDtypeStruct((m, d), F32),
        scratch_shapes=[pltpu.VMEM((tm, heads * vdim), BF16)],
        compiler_params=_cparams("parallel"),
        name="mla_post",
    )(o_lat, wuv, wo, x)


def _rot_half_cols(w, half):
    return jnp.concatenate([-w[..., half:], w[..., :half]], axis=-1)


def gla_layer(x, ln_g, w_in_t, w_gate_up, b_gate, g_norm, w_out, state_gla, states, layer, dims):
    nbp, lp, nbs, ls, heads, dk, dv, rank = dims
    nq, nv = heads * dk, heads * dv
    mp = nbp * lp
    g = ln_g.reshape(1, -1)
    lane = 128
    wa = jnp.pad(w_in_t[layer, 2 * nq + 2 * nv:], ((0, lane - rank), (0, 0))).astype(BF16)
    wg = jnp.pad(w_gate_up, ((0, lane - rank), (0, 0))).astype(BF16)
    qk = norm_matmul_t(x, g, w_in_t, layer, 0, 2 * nq, F32)
    vr = norm_matmul_t(x, g, w_in_t, layer, 2 * nq, 2 * nv, BF16)
    la = gla_gate(x, g, wa, wg, b_gate.reshape(1, -1))
    gn = g_norm.reshape(1, -1)
    sp_all, ss_all = states
    og, sp_all = gla_prompt(qk, vr, la, gn, sp_all, layer, nb=nbp, seq=lp, heads=heads, dk=dk, dv=dv,
                            n_layers=state_gla.shape[0])
    og, ss_all = gla_sample(qk, vr, la, gn, state_gla, og, ss_all, layer, row0=mp, nb=nbs, seq=ls,
                            heads=heads, dk=dk, dv=dv)
    x = matmul_res(og, w_out.astype(BF16), x)
    return x, (sp_all, ss_all)


def mla_layer(x, ln_g, w_in_t, g_q, w_qb, g_kv, w_uk, w_uv, w_out, cache_lat, cache_krt, page_table,
              cos, sin, layer, dims):
    nbp, lp, nbs, ls, heads, q_lora, kv_lora, nope, rope = dims
    mp = nbp * lp
    half = rope // 2
    o = q_lora + kv_lora
    w_kr = w_in_t[o:o + rope]
    win_ext = jnp.concatenate([w_in_t, -w_kr[half:], w_kr[:half]], axis=0).astype(BF16)
    wqb = w_qb.reshape(q_lora, heads, nope + rope)
    w_rope = wqb[:, :, nope:]
    wqb_ext = jnp.concatenate([
        wqb[:, :, :nope].reshape(q_lora, heads * nope),
        w_rope.reshape(q_lora, heads * rope),
        _rot_half_cols(w_rope, half).reshape(q_lora, heads * rope)], axis=1).astype(BF16)
    wuk_t = jnp.transpose(w_uk, (1, 2, 0)).astype(BF16)
    wuv = jnp.transpose(w_uv, (1, 0, 2)).astype(BF16)
    c_new, kr_new, cb, ct, krb, ql, qr = mla_pre(
        x, ln_g.reshape(1, -1), win_ext, g_q.reshape(1, -1), g_kv.reshape(1, -1), wqb_ext, wuk_t,
        cos, sin, heads=heads, q_lora=q_lora, kv_lora=kv_lora, nope=nope, rope=rope)
    o_lat = mla_attn_prompt(ql, qr, cb, krb, ct, nb=nbp, seq=lp)
    o_lat = mla_attn_sample(page_table, ql, qr, c_new, kr_new, cache_lat, cache_krt, o_lat, layer,
                            row0=mp, nb=nbs, seq=ls)
    x = mla_post(o_lat, wuv, w_out.astype(BF16), x)
    return x, c_new, kr_new


def kernel(x_prompt, x_sample, state_gla, cache_mla_latent, cache_mla_krope, page_table, ln_mixer_g, ln_ffn_g, gla_w_in, gla_w_gate_up, gla_b_gate, gla_norm_g, gla_w_out, mla_w_in, mla_q_norm_g, mla_w_qb, mla_kv_norm_g, mla_w_uk, mla_w_uv, mla_w_out, ffn_w_gate, ffn_w_up, ffn_w_down, final_norm_g):
    nbp, lp, d = x_prompt.shape
    nbs, ls, _ = x_sample.shape
    mp, ms = nbp * lp, nbs * ls
    depth = ln_mixer_g.shape[0]
    heads_g, dk, dv = state_gla.shape[2:]
    rank = gla_w_gate_up.shape[1]
    kv_lora, heads_m, nope = mla_w_uk.shape[1:]
    q_lora = mla_q_norm_g.shape[1]
    rope = cache_mla_krope.shape[3]
    past_len = page_table.shape[1] * cache_mla_latent.shape[2]

    x = jnp.concatenate([x_prompt.reshape(mp, d), x_sample.reshape(ms, d)], axis=0)

    half = rope // 2
    inv = 1.0 / (ROPE_THETA ** (jnp.arange(half, dtype=F32) / half))
    pos = jnp.concatenate([jnp.tile(jnp.arange(lp, dtype=jnp.int32), nbp),
                           jnp.tile(past_len + jnp.arange(ls, dtype=jnp.int32), nbs)])
    ang = pos.astype(F32)[:, None] * inv[None, :]
    cos = jnp.tile(jnp.cos(ang), (1, 2 * rope // half))
    sin = jnp.tile(jnp.sin(ang), (1, 2 * rope // half))

    gla_dims = (nbp, lp, nbs, ls, heads_g, dk, dv, rank)
    mla_dims = (nbp, lp, nbs, ls, heads_m, q_lora, kv_lora, nope, rope)
    cache_krt = jnp.swapaxes(cache_mla_krope, 2, 3)
    gla_w_in_t = jnp.swapaxes(gla_w_in, 1, 2)
    mla_w_in_t = jnp.swapaxes(mla_w_in, 1, 2)
    gla_states = (None, None)
    mla_c, mla_kr = [], []
    for i in range(depth):
        j = i // 2
        if i % 2 == 0:
            x, gla_states = gla_layer(x, ln_mixer_g[i], gla_w_in_t, gla_w_gate_up[j], gla_b_gate[j],
                                      gla_norm_g[j], gla_w_out[j], state_gla, gla_states, j, gla_dims)
        else:
            x, c_new, kr_new = mla_layer(x, ln_mixer_g[i], mla_w_in_t[j], mla_q_norm_g[j], mla_w_qb[j],
                                         mla_kv_norm_g[j], mla_w_uk[j], mla_w_uv[j], mla_w_out[j],
                                         cache_mla_latent, cache_krt, page_table, cos, sin, j,
                                         mla_dims)
            mla_c.append(c_new)
            mla_kr.append(kr_new)
        x = ffn(x, ln_ffn_g[i].reshape(1, -1), ffn_w_gate, ffn_w_up, ffn_w_down, i)

    fg = final_norm_g.reshape(1, -1)
    y_prompt = rmsnorm_rows(x, fg, 0, mp).reshape(nbp, lp, d)
    y_sample = rmsnorm_rows(x, fg, mp, ms).reshape(nbs, ls, d)
    c_all = jnp.stack(mla_c)
    kr_all = jnp.stack(mla_kr)
    return (y_prompt, y_sample, gla_states[0], gla_states[1],
            c_all[:, :mp].reshape(-1, nbp, lp, kv_lora), kr_all[:, :mp].reshape(-1, nbp, lp, rope),
            c_all[:, mp:].reshape(-1, nbs, ls, kv_lora), kr_all[:, mp:].reshape(-1, nbs, ls, rope))
```
